```python
import math
import jax, jax.numpy as jnp
from jax import lax
import numpy as np

D_MODEL = 1024
BATCH = 8
SEQ = 2048
DEPTH = 4
DEC_BATCH = 32
DEC_SEQ = 1
PAST_LEN = 8192
PAGE_SIZE = 128

N_EVEN = (DEPTH + 1) // 2
N_ODD = DEPTH // 2
D_A = D_MODEL // 2
SSM_GROUP = 16
N_GROUPS = D_A // SSM_GROUP
SSM_STATE = 64
D_B = D_MODEL // 2
N_HEADS = 8
HEAD_DIM = D_B // N_HEADS
IN_COLS = D_A + 3 * D_B + N_HEADS
Q_BLOCK = 128
D_CONV = D_MODEL
CONV_WIDTH = 31
D_FF = ((-(-8 * D_MODEL // 3)) + 255) // 256 * 256
EPS = 1e-6

kernel_name = 'hybrid_s5_fox_conformer_decode_step'


def rms_norm(x, g):
    xf = x.astype(jnp.float32)
    y = xf * lax.rsqrt(jnp.mean(jnp.square(xf), axis=-1, keepdims=True) + EPS)
    return (y * g.astype(jnp.float32)).astype(x.dtype)


def layer_norm(x, g, b):
    xf = x.astype(jnp.float32)
    mu = jnp.mean(xf, axis=-1, keepdims=True)
    var = jnp.mean(jnp.square(xf - mu), axis=-1, keepdims=True)
    y = (xf - mu) * lax.rsqrt(var + EPS)
    return (y * g.astype(jnp.float32) + b.astype(jnp.float32)).astype(x.dtype)


def swiglu_ffn(x, g, w_gate, w_up, w_down):
    h = rms_norm(x, g)
    return x + (jax.nn.silu(h @ w_gate) * (h @ w_up)) @ w_down


def _ssm_combine(e1, e2):
    a1, b1 = e1
    a2, b2 = e2
    return a1 * a2, a2 * b1 + b2


def s5_mixer(u, h0_re, h0_im, lam_re, lam_im, log_step, b_re, b_im, c_re, c_im, d_skip, w_glu, b_glu):
    bsz, length, _ = u.shape
    lam = lax.complex(lam_re.astype(jnp.float32), lam_im.astype(jnp.float32))
    step = jnp.exp(log_step.astype(jnp.float32))[:, None]
    lam_bar = jnp.exp(lam * step)
    b = lax.complex(b_re.astype(jnp.float32), b_im.astype(jnp.float32))
    b_bar = ((lam_bar - 1.0) / lam)[..., None] * b
    c = lax.complex(c_re.astype(jnp.float32), c_im.astype(jnp.float32))
    uf = u.astype(jnp.float32).reshape(bsz, length, N_GROUPS, SSM_GROUP)
    bu = jnp.einsum('gnp,blgp->blgn', b_bar, uf.astype(jnp.complex64))
    a = jnp.broadcast_to(lam_bar, bu.shape)
    a_cum, h = lax.associative_scan(_ssm_combine, (a, bu), axis=1)
    h0 = lax.complex(h0_re.astype(jnp.float32), h0_im.astype(jnp.float32))
    h = h + a_cum * h0[:, None]
    y = jnp.einsum('gpn,blgn->blgp', c, h).real + d_skip.astype(jnp.float32) * uf
    y = jax.nn.gelu(y.reshape(bsz, length, D_A)).astype(u.dtype)
    y = y * jax.nn.sigmoid(y @ w_glu + b_glu)
    h_last = h[:, -1]
    return y, h_last.real, h_last.imag


def even_in_proj(x, norm_g, w_in, b_f, q_norm, k_norm):
    h = rms_norm(x, norm_g)
    z = h @ w_in
    u, q, k, v, f = jnp.split(z, [D_A, D_A + D_B, D_A + 2 * D_B, D_A + 3 * D_B], axis=-1)
    bsz, length, _ = x.shape
    q = rms_norm(q.reshape(bsz, length, N_HEADS, HEAD_DIM), q_norm)
    k = rms_norm(k.reshape(bsz, length, N_HEADS, HEAD_DIM), k_norm)
    v = v.reshape(bsz, length, N_HEADS, HEAD_DIM)
    logf = jax.nn.log_sigmoid(f.astype(jnp.float32) + b_f.astype(jnp.float32))
    return u, q, k, v, logf


def fox_attend_prompt(q, k, v, logf):
    bsz, length, n_heads, head_dim = q.shape
    n_blocks = length // Q_BLOCK
    scale = head_dim ** -0.5
    dc = jnp.cumsum(logf, axis=1).transpose(0, 2, 1)
    q_blocks = q.reshape(bsz, n_blocks, Q_BLOCK, n_heads, head_dim).transpose(1, 0, 2, 3, 4)
    dc_blocks = dc.reshape(bsz, n_heads, n_blocks, Q_BLOCK).transpose(2, 0, 1, 3)
    key_pos = jnp.arange(length)

    def one_block(args):
        blk, q_blk, dc_blk = args
        logits = jnp.einsum('bqhd,bkhd->bhqk', q_blk, k, preferred_element_type=jnp.float32) * scale
        logits = logits + (dc_blk[..., :, None] - dc[..., None, :])
        query_pos = blk * Q_BLOCK + jnp.arange(Q_BLOCK)
        causal = key_pos[None, :] <= query_pos[:, None]
        probs = jax.nn.softmax(jnp.where(causal, logits, -jnp.inf), axis=-1)
        return jnp.einsum('bhqk,bkhd->bqhd', probs.astype(v.dtype), v)

    out = lax.map(one_block, (jnp.arange(n_blocks), q_blocks, dc_blocks))
    return out.transpose(1, 0, 2, 3, 4).reshape(bsz, length, n_heads, head_dim)


def fox_attend_sample(q, k_new, v_new, logf_new, k_past, v_past, logf_past):
    past_len = k_past.shape[1]
    n_new = q.shape[1]
    scale = q.shape[-1] ** -0.5
    k = jnp.concatenate([k_past.astype(k_new.dtype), k_new], axis=1)
    v = jnp.concatenate([v_past.astype(v_new.dtype), v_new], axis=1)
    logf = jnp.concatenate([logf_past.astype(jnp.float32), logf_new], axis=1)
    dc = jnp.cumsum(logf, axis=1).transpose(0, 2, 1)
    logits = jnp.einsum('bqhd,bkhd->bhqk', q, k, preferred_element_type=jnp.float32) * scale
    logits = logits + (dc[..., past_len:, None] - dc[..., None, :])
    causal = jnp.arange(past_len + n_new)[None, :] <= (past_len + jnp.arange(n_new))[:, None]
    probs = jax.nn.softmax(jnp.where(causal, logits, -jnp.inf), axis=-1)
    return jnp.einsum('bhqk,bkhd->bqhd', probs.astype(v.dtype), v)


def even_out_proj(s5_out, attn_out, w_out):
    bsz, length = s5_out.shape[:2]
    merged = jnp.concatenate([s5_out, attn_out.reshape(bsz, length, D_B).astype(s5_out.dtype)], axis=-1)
    return merged @ w_out


def conformer_conv(h, hist, w_pw1, b_pw1, w_dw, b_dw, ln_g, ln_b, w_pw2, b_pw2):
    a, gate = jnp.split(h @ w_pw1 + b_pw1, 2, axis=-1)
    g = a * jax.nn.sigmoid(gate)
    full = jnp.concatenate([hist.astype(g.dtype), g], axis=1)
    y = lax.conv_general_dilated(full, w_dw[:, None, :].astype(g.dtype), window_strides=(1,),
                                 padding='VALID', dimension_numbers=('NWC', 'WIO', 'NWC'),
                                 feature_group_count=D_CONV) + b_dw
    y = jax.nn.silu(layer_norm(y, ln_g, ln_b))
    return y @ w_pw2 + b_pw2, full[:, full.shape[1] - (CONV_WIDTH - 1):]


def setup_inputs(seed: int = 0) -> dict:
    key = jax.random.key(seed)
    ks = iter(jax.random.split(key, 48))
    f32 = jnp.float32

    def nrm(shape, scale):
        return scale * jax.random.normal(next(ks), shape, f32)

    n_pages = PAST_LEN // PAGE_SIZE
    n_used = DEC_BATCH * n_pages
    n_pool = n_used + max(1, n_used // 4)
    page_table = jax.random.permutation(next(ks), n_pool)[:n_used].reshape(DEC_BATCH, n_pages).astype(jnp.int32)

    inputs = {}
    inputs['x_prompt'] = nrm((BATCH, SEQ, D_MODEL), 1.0)
    inputs['x_sample'] = nrm((DEC_BATCH, DEC_SEQ, D_MODEL), 1.0)
    inputs['cache_k'] = nrm((N_EVEN, n_pool, PAGE_SIZE, N_HEADS, HEAD_DIM), 1.0)
    inputs['cache_v'] = nrm((N_EVEN, n_pool, PAGE_SIZE, N_HEADS, HEAD_DIM), 1.0)
    inputs['cache_logf'] = jax.nn.log_sigmoid(2.5 + nrm((N_EVEN, n_pool, PAGE_SIZE, N_HEADS), 1.0))
    inputs['page_table'] = page_table
    inputs['state_ssm_re'] = nrm((N_EVEN, DEC_BATCH, N_GROUPS, SSM_STATE), 0.1)
    inputs['state_ssm_im'] = nrm((N_EVEN, DEC_BATCH, N_GROUPS, SSM_STATE), 0.1)
    inputs['state_conv'] = nrm((N_ODD, DEC_BATCH, CONV_WIDTH - 1, D_CONV), 0.5)
    inputs['norm_mix'] = 1.0 + nrm((DEPTH, D_MODEL), 0.02)
    inputs['norm_ffn'] = 1.0 + nrm((DEPTH, D_MODEL), 0.02)
    inputs['w_in'] = nrm((N_EVEN, D_MODEL, IN_COLS), D_MODEL ** -0.5)
    inputs['b_f'] = jnp.linspace(1.0, 4.0, N_HEADS, dtype=f32) + nrm((N_EVEN, N_HEADS), 0.1)
    inputs['q_norm'] = 1.0 + nrm((N_EVEN, HEAD_DIM), 0.02)
    inputs['k_norm'] = 1.0 + nrm((N_EVEN, HEAD_DIM), 0.02)
    inputs['lam_re'] = -0.5 + nrm((N_EVEN, N_GROUPS, SSM_STATE), 0.01)
    inputs['lam_im'] = math.pi * jnp.arange(SSM_STATE, dtype=f32) + nrm((N_EVEN, N_GROUPS, SSM_STATE), 0.01)
    inputs['log_step'] = jax.random.uniform(next(ks), (N_EVEN, N_GROUPS), f32, math.log(1e-3), math.log(1e-1))
    inputs['b_re'] = nrm((N_EVEN, N_GROUPS, SSM_STATE, SSM_GROUP), (2 * SSM_GROUP) ** -0.5)
    inputs['b_im'] = nrm((N_EVEN, N_GROUPS, SSM_STATE, SSM_GROUP), (2 * SSM_GROUP) ** -0.5)
    inputs['c_re'] = nrm((N_EVEN, N_GROUPS, SSM_GROUP, SSM_STATE), SSM_STATE ** -0.5)
    inputs['c_im'] = nrm((N_EVEN, N_GROUPS, SSM_GROUP, SSM_STATE), SSM_STATE ** -0.5)
    inputs['d_skip'] = nrm((N_EVEN, N_GROUPS, SSM_GROUP), 1.0)
    inputs['w_glu'] = nrm((N_EVEN, D_A, D_A), D_A ** -0.5)
    inputs['b_glu'] = nrm((N_EVEN, D_A), 0.01)
    inputs['w_out'] = nrm((N_EVEN, D_A + D_B, D_MODEL), (D_A + D_B) ** -0.5)
    inputs['w_pw1'] = nrm((N_ODD, D_MODEL, 2 * D_CONV), D_MODEL ** -0.5)
    inputs['b_pw1'] = nrm((N_ODD, 2 * D_CONV), 0.01)
    inputs['w_dw'] = nrm((N_ODD, CONV_WIDTH, D_CONV), CONV_WIDTH ** -0.5)
    inputs['b_dw'] = nrm((N_ODD, D_CONV), 0.01)
    inputs['ln_g'] = 1.0 + nrm((N_ODD, D_CONV), 0.02)
    inputs['ln_b'] = nrm((N_ODD, D_CONV), 0.01)
    inputs['w_pw2'] = nrm((N_ODD, D_CONV, D_MODEL), D_CONV ** -0.5)
    inputs['b_pw2'] = nrm((N_ODD, D_MODEL), 0.01)
    inputs['w_gate'] = nrm((DEPTH, D_MODEL, D_FF), D_MODEL ** -0.5)
    inputs['w_up'] = nrm((DEPTH, D_MODEL, D_FF), D_MODEL ** -0.5)
    inputs['w_down'] = nrm((DEPTH, D_FF, D_MODEL), D_FF ** -0.5)
    return inputs


def reference(x_prompt, x_sample, cache_k, cache_v, cache_logf, page_table, state_ssm_re, state_ssm_im,
              state_conv, norm_mix, norm_ffn, w_in, b_f, q_norm, k_norm, lam_re, lam_im, log_step,
              b_re, b_im, c_re, c_im, d_skip, w_glu, b_glu, w_out, w_pw1, b_pw1, w_dw, b_dw, ln_g, ln_b,
              w_pw2, b_pw2, w_gate, w_up, w_down):
    yp, ys = x_prompt, x_sample
    n_prompt = x_prompt.shape[0]
    n_dec = x_sample.shape[0]
    kp_l, vp_l, lfp_l, srp_l, sip_l, cvp_l = [], [], [], [], [], []
    ks_l, vs_l, lfs_l, srs_l, sis_l, cvs_l = [], [], [], [], [], []
    for layer in range(DEPTH):
        if layer % 2 == 0:
            e = layer // 2
            s5_w = (lam_re[e], lam_im[e], log_step[e], b_re[e], b_im[e], c_re[e], c_im[e],
                    d_skip[e], w_glu[e], b_glu[e])
            u, q, k, v, logf = even_in_proj(yp, norm_mix[layer], w_in[e], b_f[e], q_norm[e], k_norm[e])
            h0 = jnp.zeros((n_prompt, N_GROUPS, SSM_STATE), jnp.float32)
            s_out, h_re, h_im = s5_mixer(u, h0, h0, *s5_w)
            a_out = fox_attend_prompt(q, k, v, logf)
            yp = yp + even_out_proj(s_out, a_out, w_out[e])
            kp_l.append(k); vp_l.append(v); lfp_l.append(logf); srp_l.append(h_re); sip_l.append(h_im)
            u, q, k, v, logf = even_in_proj(ys, norm_mix[layer], w_in[e], b_f[e], q_norm[e], k_norm[e])
            s_out, h_re, h_im = s5_mixer(u, state_ssm_re[e], state_ssm_im[e], *s5_w)
            k_past = cache_k[e][page_table].reshape(n_dec, -1, N_HEADS, HEAD_DIM)
            v_past = cache_v[e][page_table].reshape(n_dec, -1, N_HEADS, HEAD_DIM)
            lf_past = cache_logf[e][page_table].reshape(n_dec, -1, N_HEADS)
            a_out = fox_attend_sample(q, k, v, logf, k_past, v_past, lf_past)
            ys = ys + even_out_proj(s_out, a_out, w_out[e])
            ks_l.append(k); vs_l.append(v); lfs_l.append(logf); srs_l.append(h_re); sis_l.append(h_im)
        else:
            o = layer // 2
            cw = (w_pw1[o], b_pw1[o], w_dw[o], b_dw[o], ln_g[o], ln_b[o], w_pw2[o], b_pw2[o])
            hist0 = jnp.zeros((n_prompt, CONV_WIDTH - 1, D_CONV), yp.dtype)
            c_out, hist = conformer_conv(rms_norm(yp, norm_mix[layer]), hist0, *cw)
            yp = yp + c_out
            cvp_l.append(hist)
            c_out, hist = conformer_conv(rms_norm(ys, norm_mix[layer]), state_conv[o], *cw)
            ys = ys + c_out
            cvs_l.append(hist)
        yp = swiglu_ffn(yp, norm_ffn[layer], w_gate[layer], w_up[layer], w_down[layer])
        ys = swiglu_ffn(ys, norm_ffn[layer], w_gate[layer], w_up[layer], w_down[layer])
    return (yp, ys,
            jnp.stack(kp_l), jnp.stack(vp_l), jnp.stack(lfp_l), jnp.stack(srp_l), jnp.stack(sip_l), jnp.stack(cvp_l),
            jnp.stack(ks_l), jnp.stack(vs_l), jnp.stack(lfs_l), jnp.stack(srs_l), jnp.stack(sis_l), jnp.stack(cvs_l))
```

```python
import functools
import math

import numpy as np
import jax
import jax.numpy as jnp
from jax import lax
from jax.experimental import pallas as pl
from jax.experimental.pallas import tpu as pltpu

F32 = jnp.float32
BF16 = jnp.bfloat16
EPS = 1e-6
MASKED = -1e30
LANES = 128
VMEM_LIMIT = 56 * 1024 * 1024
SQRT_2_OVER_PI = float(np.float32(math.sqrt(2.0 / math.pi)))


def _dot(a, b):
    return jnp.dot(a, b, preferred_element_type=F32)


def _dot_nt(a, b):
    return lax.dot_general(a, b, (((1,), (1,)), ((), ())), preferred_element_type=F32)


def _sigmoid(x):
    return 1.0 / (1.0 + jnp.exp(-x))


def _rms_rows(x, g):
    return x * lax.rsqrt(jnp.mean(x * x, axis=-1, keepdims=True) + EPS) * g


def _params(*sem):
    return pltpu.CompilerParams(dimension_semantics=sem, vmem_limit_bytes=VMEM_LIMIT)


def _resident(shape):
    zeros = (0,) * len(shape)
    return pl.BlockSpec(shape, lambda *_: zeros, pipeline_mode=pl.Buffered(1))


def _inproj_kernel(x_ref, g_ref, wuq_ref, wkvT_ref, wfT_ref, bf_ref, qn_ref, kn_ref, seg_ref,
                   u_ref, q_ref, kT_ref, vT_ref, kTb_ref, vTb_ref, lfT_ref, *, n_heads):
    h = _rms_rows(x_ref[...], g_ref[...]).astype(BF16)
    z = _dot(h, wuq_ref[...])
    d = q_ref.shape[1]
    u_ref[...] = z[:, :z.shape[1] - d]
    q = z[:, z.shape[1] - d:]
    q_ms = _dot((q * q).astype(BF16), seg_ref[...])
    q_ref[...] = (q * lax.rsqrt(q_ms + EPS) * qn_ref[...]).astype(BF16)
    zT = _dot_nt(wkvT_ref[...], h)
    tm = zT.shape[1]
    k3 = zT[:d].reshape(n_heads, d // n_heads, tm)
    k3 = k3 * lax.rsqrt(jnp.mean(k3 * k3, axis=1, keepdims=True) + EPS) * kn_ref[...][None]
    kT = k3.reshape(d, tm)
    kT_ref[...] = kT
    kTb_ref[...] = kT.astype(BF16)
    vT = zT[d:]
    vT_ref[...] = vT
    vTb_ref[...] = vT.astype(BF16)
    f = _dot_nt(wfT_ref[...], h)[:n_heads] + bf_ref[...]
    lfT_ref[...] = jnp.minimum(f, 0.0) - jnp.log1p(jnp.exp(-jnp.abs(f)))


def _in_proj(x2d, nb, tm, g, wuq, wkvT, wfT, bf_col, qn_row, kn_col, seg, n_heads):
    rows, dm = x2d.shape
    lr = rows // nb
    nt = lr // tm
    d = qn_row.shape[1]
    da = wuq.shape[1] - d
    out_shape = (
        jax.ShapeDtypeStruct((lr, nb * da), F32),
        jax.ShapeDtypeStruct((rows, d), BF16),
        jax.ShapeDtypeStruct((nb, d, lr), F32),
        jax.ShapeDtypeStruct((nb, d, lr), F32),
        jax.ShapeDtypeStruct((nb, d, lr), BF16),
        jax.ShapeDtypeStruct((nb, d, lr), BF16),
        jax.ShapeDtypeStruct((nb, n_heads, lr), F32),
    )
    t_spec = pl.BlockSpec((None, d, tm), lambda b, t: (b, 0, t))
    return pl.pallas_call(
        functools.partial(_inproj_kernel, n_heads=n_heads),
        grid=(nb, nt),
        in_specs=[
            pl.BlockSpec((tm, dm), lambda b, t: (b * nt + t, 0)),
            _resident(g.shape), _resident(wuq.shape), _resident(wkvT.shape), _resident(wfT.shape),
            _resident(bf_col.shape), _resident(qn_row.shape), _resident(kn_col.shape), _resident(seg.shape),
        ],
        out_specs=(
            pl.BlockSpec((tm, da), lambda b, t: (t, b)),
            pl.BlockSpec((tm, d), lambda b, t: (b * nt + t, 0)),
            t_spec, t_spec, t_spec, t_spec,
            pl.BlockSpec((None, n_heads, tm), lambda b, t: (b, 0, t)),
        ),
        out_shape=out_shape,
        compiler_params=_params("parallel", "parallel"),
        name="in_proj",
    )(x2d, g, wuq, wkvT, wfT, bf_col, qn_row, kn_col, seg)


def _s5_kernel(u_ref, h0r_ref, h0i_ref, wb_ref, lbr_ref, lbi_ref, wc_ref, dsk_ref, wglu_ref, bglu_ref,
               s_ref, hr_out, hi_out, hbuf, hst_r, hst_i, *, bsz, tl):
    nblk, cb, sc2 = wb_ref.shape
    sc = sc2 // 2

    @pl.when(pl.program_id(0) == 0)
    def _():
        hst_r[...] = h0r_ref[...]
        hst_i[...] = h0i_ref[...]

    u = u_ref[...]
    for j in range(nblk):
        hbuf[:, sc2 * j:sc2 * (j + 1)] = _dot(u[:, cb * j:cb * (j + 1)].astype(BF16), wb_ref[j])

    for j in range(nblk):
        lre = jnp.broadcast_to(lbr_ref[:, sc * j:sc * (j + 1)], (bsz, sc))
        lim = jnp.broadcast_to(lbi_ref[:, sc * j:sc * (j + 1)], (bsz, sc))
        c_re, c_im = sc2 * j, sc2 * j + sc

        def step(t, carry, lre=lre, lim=lim, c_re=c_re, c_im=c_im):
            hr, hi = carry
            r0 = pl.multiple_of(t * bsz, bsz)
            nr = lre * hr - lim * hi + hbuf[pl.ds(r0, bsz), c_re:c_re + sc]
            ni = lre * hi + lim * hr + hbuf[pl.ds(r0, bsz), c_im:c_im + sc]
            hbuf[pl.ds(r0, bsz), c_re:c_re + sc] = nr
            hbuf[pl.ds(r0, bsz), c_im:c_im + sc] = ni
            return nr, ni

        hr, hi = lax.fori_loop(0, tl, step, (hst_r[:, sc * j:sc * (j + 1)], hst_i[:, sc * j:sc * (j + 1)]),
                               unroll=min(tl, 4))
        hst_r[:, sc * j:sc * (j + 1)] = hr
        hst_i[:, sc * j:sc * (j + 1)] = hi

    ys = []
    for j in range(nblk):
        hj = hbuf[:, sc2 * j:sc2 * (j + 1)].astype(BF16)
        ys.append(_dot(hj, wc_ref[j]) + dsk_ref[:, cb * j:cb * (j + 1)] * u[:, cb * j:cb * (j + 1)])
    y = jnp.concatenate(ys, axis=1)
    y = 0.5 * y * (1.0 + jnp.tanh(SQRT_2_OVER_PI * (y + 0.044715 * (y * y * y))))
    z = _dot(y.astype(BF16), wglu_ref[...]) + bglu_ref[...]
    s_ref[...] = (y * _sigmoid(z)).astype(BF16)
    hr_out[...] = hst_r[...]
    hi_out[...] = hst_i[...]


def _s5(u_tm, h0r, h0i, wb, lbr, lbi, wc, dsk, wglu, bglu, bsz, tl):
    rows, da = u_tm.shape
    nstate = h0r.shape[1]
    steps = rows // (tl * bsz)
    blk = tl * bsz
    return pl.pallas_call(
        functools.partial(_s5_kernel, bsz=bsz, tl=tl),
        grid=(steps,),
        in_specs=[
            pl.BlockSpec((blk, da), lambda t: (t, 0)),
            _resident(h0r.shape), _resident(h0i.shape), _resident(wb.shape), _resident(lbr.shape),
            _resident(lbi.shape), _resident(wc.shape), _resident(dsk.shape), _resident(wglu.shape),
            _resident(bglu.shape),
        ],
        out_specs=(
            pl.BlockSpec((blk, da), lambda t: (t, 0)),
            pl.BlockSpec(h0r.shape, lambda t: (0, 0)),
            pl.BlockSpec(h0i.shape, lambda t: (0, 0)),
        ),
        out_shape=(
            jax.ShapeDtypeStruct((rows, da), BF16),
            jax.ShapeDtypeStruct(h0r.shape, F32),
            jax.ShapeDtypeStruct(h0i.shape, F32),
        ),
        scratch_shapes=[
            pltpu.VMEM((blk, 2 * nstate), F32),
            pltpu.VMEM(h0r.shape, F32),
            pltpu.VMEM(h0i.shape, F32),
        ],
        compiler_params=_params("arbitrary"),
        name="s5_mixer",
    )(u_tm, h0r, h0i, wb, lbr, lbi, wc, dsk, wglu, bglu)


def _s5_weights(lam_re, lam_im, log_step, b_re, b_im, c_re, c_im, d_skip):
    n_groups, n_state = lam_re.shape
    p = b_re.shape[-1]
    gpb = LANES // p
    nblk = n_groups // gpb
    step = jnp.exp(log_step)[:, None]
    mag = jnp.exp(lam_re * step)
    lbr = mag * jnp.cos(lam_im * step)
    lbi = mag * jnp.sin(lam_im * step)
    den = lam_re * lam_re + lam_im * lam_im
    fr = ((lbr - 1.0) * lam_re + lbi * lam_im) / den
    fi = (lbi * lam_re - (lbr - 1.0) * lam_im) / den
    bbr = fr[..., None] * b_re - fi[..., None] * b_im
    bbi = fr[..., None] * b_im + fi[..., None] * b_re
    eye = jnp.eye(gpb, dtype=F32)

    def b_block(bb):
        t = bb.reshape(nblk, gpb, n_state, p)
        return jnp.einsum('jgnp,gh->jgphn', t, eye).reshape(nblk, gpb * p, gpb * n_state)

    def c_block(cc):
        t = cc.reshape(nblk, gpb, p, n_state)
        return jnp.einsum('jgpn,gh->jgnhp', t, eye).reshape(nblk, gpb * n_state, gpb * p)

    wb = jnp.concatenate([b_block(bbr), b_block(bbi)], axis=-1).astype(BF16)
    wc = jnp.concatenate([c_block(c_re), -c_block(c_im)], axis=1).astype(BF16)
    return (wb, lbr.reshape(1, n_groups * n_state), lbi.reshape(1, n_groups * n_state), wc,
            d_skip.reshape(1, n_groups * p))


def _cumsum_kernel(x_ref, o_ref):
    rows, length = x_ref.shape
    lane = lax.broadcasted_iota(jnp.int32, (rows, LANES), 1)
    carry = jnp.zeros((rows, 1), F32)
    for c in range(length // LANES):
        x = x_ref[:, LANES * c:LANES * (c + 1)]
        d = 1
        while d < LANES:
            x = x + jnp.where(lane >= d, pltpu.roll(x, d, 1), 0.0)
            d *= 2
        x = x + carry
        o_ref[:, LANES * c:LANES * (c + 1)] = x
        carry = x[:, LANES - 1:LANES]


def _cumsum_rows(x):
    return pl.pallas_call(
        _cumsum_kernel,
        out_shape=jax.ShapeDtypeStruct(x.shape, F32),
        name="logf_cumsum",
    )(x)


def _attn_kernel(q_ref, kT_ref, vT_ref, dc_ref, o_ref, *, blk, hd):
    hp = pl.program_id(1)
    qi = pl.program_id(2)
    q = q_ref[...]
    lane = lax.broadcasted_iota(jnp.int32, (1, 2 * hd), 1)
    row = lax.broadcasted_iota(jnp.int32, (blk, blk), 0)
    col = lax.broadcasted_iota(jnp.int32, (blk, blk), 1)
    outs = []
    for hh in range(2):
        in_head = (lane >= hd) if hh else (lane < hd)
        qm = jnp.where(in_head, q, jnp.zeros_like(q))

        def step(kb, carry, diag, qm=qm, hh=hh):
            m, l, acc = carry
            ks = pl.multiple_of(kb * blk, blk)
            s = _dot(qm, kT_ref[:, pl.ds(ks, blk)])
            t = s - dc_ref[pl.ds(2 * hp + hh, 1), pl.ds(ks, blk)]
            if diag:
                t = jnp.where(col <= row, t, MASKED)
            m_new = jnp.maximum(m, jnp.max(t, axis=-1, keepdims=True))
            alpha = jnp.exp(m - m_new)
            p = jnp.exp(t - m_new)
            l = alpha * l + jnp.sum(p, axis=-1, keepdims=True)
            acc = alpha * acc + _dot_nt(p.astype(BF16), vT_ref[:, pl.ds(ks, blk)])
            return m_new, l, acc

        init = (jnp.full((blk, 1), MASKED, F32), jnp.zeros((blk, 1), F32), jnp.zeros((blk, 2 * hd), F32))
        carry = lax.fori_loop(0, qi, functools.partial(step, diag=False), init)
        _, l, acc = step(qi, carry, True)
        outs.append(acc / l)
    o_ref[...] = jnp.where(lane < hd, outs[0], outs[1]).astype(o_ref.dtype)


def _attn_prompt(q, kTb, vTb, dc, blk):
    nb, d, length = kTb.shape
    n_heads = dc.shape[1]
    hd = d // n_heads
    nq = length // blk
    return pl.pallas_call(
        functools.partial(_attn_kernel, blk=blk, hd=hd),
        grid=(nb, n_heads // 2, nq),
        in_specs=[
            pl.BlockSpec((blk, 2 * hd), lambda b, hp, qi: (b * nq + qi, hp)),
            pl.BlockSpec((None, 2 * hd, length), lambda b, hp, qi: (b, hp, 0)),
            pl.BlockSpec((None, 2 * hd, length), lambda b, hp, qi: (b, hp, 0)),
            pl.BlockSpec((None, n_heads, length), lambda b, hp, qi: (b, 0, 0)),
        ],
        out_specs=pl.BlockSpec((blk, 2 * hd), lambda b, hp, qi: (b * nq + qi, hp)),
        out_shape=jax.ShapeDtypeStruct(q.shape, BF16),
        compiler_params=_params("parallel", "parallel", "parallel"),
        name="fox_prompt",
    )(q, kTb, vTb, dc)


def _decode_kernel(pt_ref, q_ref, kn_ref, vn_ref, lfn_ref, *refs, pages_per_step, hd):
    del pt_ref
    g = pages_per_step
    k_refs, v_refs, lf_refs = refs[:g], refs[g:2 * g], refs[2 * g:3 * g]
    o_ref, q8_ref, m_ref, l_ref, acc_ref, suf_ref = refs[3 * g:]
    n_heads, d = q8_ref.shape
    s_idx = pl.program_id(1)
    head_of_lane = lax.broadcasted_iota(jnp.int32, (n_heads, d), 1) // hd
    head_of_row = lax.broadcasted_iota(jnp.int32, (n_heads, d), 0)
    own = head_of_lane == head_of_row

    @pl.when(s_idx == 0)
    def _():
        q8 = jnp.where(own, jnp.broadcast_to(q_ref[0], (n_heads, d)), 0.0)
        q8_ref[...] = q8.astype(BF16)
        m_ref[...] = jnp.sum(q8 * kn_ref[0], axis=-1, keepdims=True)
        l_ref[...] = jnp.ones(l_ref.shape, F32)
        acc_ref[...] = jnp.broadcast_to(vn_ref[0], (n_heads, d))
        suf_ref[...] = lfn_ref[0]

    lane = lax.broadcasted_iota(jnp.int32, (n_heads, LANES), 1)
    q8 = q8_ref[...]
    m, l, acc, suf = m_ref[...], l_ref[...], acc_ref[...], suf_ref[...]
    for i in range(g):
        lf = lf_refs[i][0, 0]
        x = lf
        sh = 1
        while sh < LANES:
            x = x + jnp.where(lane < LANES - sh, pltpu.roll(x, LANES - sh, 1), 0.0)
            sh *= 2
        bias = (x - lf) + suf
        suf = suf + x[:, 0:1]
        kt = k_refs[i][0, 0].reshape(d, LANES).astype(BF16)
        s = _dot(q8, kt) + bias
        m_new = jnp.maximum(m, jnp.max(s, axis=-1, keepdims=True))
        alpha = jnp.exp(m - m_new)
        p = jnp.exp(s - m_new)
        l = alpha * l + jnp.sum(p, axis=-1, keepdims=True)
        vt = v_refs[i][0, 0].reshape(d, LANES).astype(BF16)
        acc = alpha * acc + _dot_nt(p.astype(BF16), vt)
        m = m_new
    m_ref[...], l_ref[...], acc_ref[...], suf_ref[...] = m, l, acc, suf

    @pl.when(s_idx == pl.num_programs(1) - 1)
    def _():
        o_ref[0] = jnp.sum(jnp.where(own, acc / l, 0.0), axis=0, keepdims=True).astype(o_ref.dtype)


def _attn_decode(q, k_new, v_new, lf_new, ck_t, cv_t, clf_t, page_table, layer, pages_per_step):
    db, d = q.shape
    n_heads, hd, page = ck_t.shape[2:]
    n_pages = page_table.shape[1]
    g = pages_per_step
    steps = n_pages // g

    def page_of(b, s, pt, i):
        return pt[b, n_pages - 1 - (s * g + i)]

    kv_specs = [pl.BlockSpec((1, 1, n_heads, hd, page),
                             lambda b, s, pt, i=i: (layer, page_of(b, s, pt, i), 0, 0, 0)) for i in range(g)]
    lf_specs = [pl.BlockSpec((1, 1, n_heads, page),
                             lambda b, s, pt, i=i: (layer, page_of(b, s, pt, i), 0, 0)) for i in range(g)]
    row_spec = pl.BlockSpec((1, 1, d), lambda b, s, pt: (b, 0, 0))
    grid_spec = pltpu.PrefetchScalarGridSpec(
        num_scalar_prefetch=1,
        grid=(db, steps),
        in_specs=[row_spec, row_spec, row_spec, pl.BlockSpec((1, n_heads, 1), lambda b, s, pt: (b, 0, 0))]
                 + kv_specs + kv_specs + lf_specs,
        out_specs=row_spec,
        scratch_shapes=[
            pltpu.VMEM((n_heads, d), BF16),
            pltpu.VMEM((n_heads, 1), F32),
            pltpu.VMEM((n_heads, 1), F32),
            pltpu.VMEM((n_heads, d), F32),
            pltpu.VMEM((n_heads, 1), F32),
        ],
    )
    out = pl.pallas_call(
        functools.partial(_decode_kernel, pages_per_step=g, hd=hd),
        grid_spec=grid_spec,
        out_shape=jax.ShapeDtypeStruct((db, 1, d), BF16),
        compiler_params=_params("parallel", "arbitrary"),
        name="fox_decode",
    )(page_table, q.reshape(db, 1, d), k_new.reshape(db, 1, d), v_new.reshape(db, 1, d),
      lf_new.reshape(db, n_heads, 1), *([ck_t] * g), *([cv_t] * g), *([clf_t] * g))
    return out.reshape(db, d)


def _pw1_kernel(x_ref, g_ref, w_ref, b_ref, o_ref):
    h = _rms_rows(x_ref[...], g_ref[...]).astype(BF16)
    z = _dot(h, w_ref[...]) + b_ref[...]
    d = o_ref.shape[1]
    o_ref[...] = z[:, :d] * _sigmoid(z[:, d:])


def _pw1(x2d, tm, g, w, b):
    rows, dm = x2d.shape
    dc = w.shape[1] // 2
    return pl.pallas_call(
        _pw1_kernel,
        grid=(rows // tm,),
        in_specs=[pl.BlockSpec((tm, dm), lambda i: (i, 0)), _resident(g.shape), _resident(w.shape),
                  _resident(b.shape)],
        out_specs=pl.BlockSpec((tm, dc), lambda i: (i, 0)),
        out_shape=jax.ShapeDtypeStruct((rows, dc), F32),
        compiler_params=_params("parallel"),
        name="conv_pw1_glu",
    )(x2d, g, w, b)


def _ln_swish(y, g, b):
    mu = jnp.mean(y, axis=-1, keepdims=True)
    yc = y - mu
    y = yc * lax.rsqrt(jnp.mean(yc * yc, axis=-1, keepdims=True) + EPS) * g + b
    return y * _sigmoid(y)


def _conv_kernel(g_ref, w_ref, bdw_ref, lng_ref, lnb_ref, o_ref, buf, *, tl, rc, width, head):
    t = pl.program_id(1)
    dch = g_ref.shape[1]

    @pl.when(t == 0)
    def _():
        buf[0:head, :] = jnp.zeros((head, dch), F32)

    @pl.when(t > 0)
    def _():
        buf[0:head, :] = buf[tl:tl + head, :]

    buf[head:head + tl, :] = g_ref[...]
    off = head - (width - 1)
    for r in range(0, tl, rc):
        acc = jnp.broadcast_to(bdw_ref[...], (rc, dch))
        for k in range(width):
            acc = acc + w_ref[k:k + 1, :] * buf[r + k + off:r + k + off + rc, :]
        o_ref[r:r + rc, :] = _ln_swish(acc, lng_ref[...], lnb_ref[...]).astype(o_ref.dtype)


def _conv_prompt(g2d, nb, tl, w_pad, bdw, lng, lnb, width):
    rows, dch = g2d.shape
    nt = rows // nb // tl
    head = -(-(width - 1) // 8) * 8
    return pl.pallas_call(
        functools.partial(_conv_kernel, tl=tl, rc=16, width=width, head=head),
        grid=(nb, nt),
        in_specs=[pl.BlockSpec((tl, dch), lambda b, t: (b * nt + t, 0)), _resident(w_pad.shape),
                  _resident(bdw.shape), _resident(lng.shape), _resident(lnb.shape)],
        out_specs=pl.BlockSpec((tl, dch), lambda b, t: (b * nt + t, 0)),
        out_shape=jax.ShapeDtypeStruct((rows, dch), BF16),
        scratch_shapes=[pltpu.VMEM((tl + head, dch), F32)],
        compiler_params=_params("parallel", "arbitrary"),
        name="conv_dw_prompt",
    )(g2d, w_pad, bdw, lng, lnb)


def _conv_step_kernel(hist_ref, g_ref, w_ref, bdw_ref, lng_ref, lnb_ref, o_ref, *, width):
    acc = bdw_ref[...] + w_ref[width - 1:width, :] * g_ref[...]
    for k in range(width - 1):
        acc = acc + w_ref[k:k + 1, :] * hist_ref[k]
    o_ref[...] = _ln_swish(acc, lng_ref[...], lnb_ref[...]).astype(o_ref.dtype)


def _conv_sample(hist_t, g2d, w_pad, bdw, lng, lnb, width):
    return pl.pallas_call(
        functools.partial(_conv_step_kernel, width=width),
        out_shape=jax.ShapeDtypeStruct(g2d.shape, BF16),
        compiler_params=pltpu.CompilerParams(vmem_limit_bytes=VMEM_LIMIT),
        name="conv_dw_sample",
    )(hist_t, g2d, w_pad, bdw, lng, lnb)


def _mixffn_kernel(*refs, n_mix, has_bias, ff_chunks):
    x_ref = refs[0]
    acts = refs[1:1 + n_mix]
    wo_ref = refs[1 + n_mix]
    pos = 2 + n_mix
    bias_ref = refs[pos] if has_bias else None
    pos += int(has_bias)
    gn_ref, wg_ref, wu_ref, wd_ref, o_ref = refs[pos:pos + 5]
    mixed = acts[0][...] if n_mix == 1 else jnp.concatenate([a[...] for a in acts], axis=1)
    x1 = x_ref[...] + _dot(mixed, wo_ref[...])
    if has_bias:
        x1 = x1 + bias_ref[...]
    h = _rms_rows(x1, gn_ref[...]).astype(BF16)
    acc = x1
    for c0, c1 in ff_chunks:
        gate = _dot(h, wg_ref[:, c0:c1])
        up = _dot(h, wu_ref[:, c0:c1])
        acc = acc + _dot((gate * _sigmoid(gate) * up).astype(BF16), wd_ref[c0:c1, :])
    o_ref[...] = acc


def _ff_chunks(dff, width=1024):
    return tuple((c, min(c + width, dff)) for c in range(0, dff, width))


def _mix_ffn(x2d, nb, tm, mix_acts, wo, bias, gn, wg, wu, wd):
    rows, dm = x2d.shape
    lr = rows // nb
    nt = lr // tm
    in_specs = [pl.BlockSpec((tm, dm), lambda b, t: (b * nt + t, 0))]
    for a in mix_acts:
        if a.shape[0] == rows:
            in_specs.append(pl.BlockSpec((tm, a.shape[1]), lambda b, t: (b * nt + t, 0)))
        else:
            in_specs.append(pl.BlockSpec((tm, a.shape[1] // nb), lambda b, t: (t, b)))
    consts = [wo] + ([bias] if bias is not None else []) + [gn, wg, wu, wd]
    in_specs += [_resident(c.shape) for c in consts]
    return pl.pallas_call(
        functools.partial(_mixffn_kernel, n_mix=len(mix_acts), has_bias=bias is not None,
                          ff_chunks=_ff_chunks(wg.shape[1])),
        grid=(nb, nt),
        in_specs=in_specs,
        out_specs=pl.BlockSpec((tm, dm), lambda b, t: (b * nt + t, 0)),
        out_shape=jax.ShapeDtypeStruct((rows, dm), F32),
        compiler_params=_params("parallel", "parallel"),
        name="mix_ffn",
    )(x2d, *mix_acts, *consts)


def kernel(x_prompt, x_sample, cache_k, cache_v, cache_logf, page_table, state_ssm_re, state_ssm_im, state_conv, norm_mix, norm_ffn, w_in, b_f, q_norm, k_norm, lam_re, lam_im, log_step, b_re, b_im, c_re, c_im, d_skip, w_glu, b_glu, w_out, w_pw1, b_pw1, w_dw, b_dw, ln_g, ln_b, w_pw2, b_pw2, w_gate, w_up, w_down):
    nb, length, dm = x_prompt.shape
    db = x_sample.shape[0]
    depth = norm_mix.shape[0]
    n_heads, hd = cache_k.shape[3], cache_k.shape[4]
    d_b = n_heads * hd
    d_a = w_in.shape[2] - 3 * d_b - n_heads
    n_state = state_ssm_re.shape[2] * state_ssm_re.shape[3]
    width = w_dw.shape[1]
    tm_p = min(512, length)
    blk_attn = min(256, length)
    tl_s5 = min(64, length)
    tl_conv = min(128, length)
    pages_per_step = min(8, page_table.shape[1])

    ck_t = jnp.transpose(cache_k, (0, 1, 3, 4, 2))
    cv_t = jnp.transpose(cache_v, (0, 1, 3, 4, 2))
    clf_t = jnp.transpose(cache_logf, (0, 1, 3, 2))

    seg = jnp.kron(jnp.eye(n_heads, dtype=F32), jnp.full((hd, hd), 1.0 / hd, F32)).astype(BF16)
    row = lambda v: v.reshape(1, -1).astype(F32)

    yp = x_prompt.reshape(nb * length, dm)
    ys = x_sample.reshape(db, dm)
    outs = {k: [] for k in ("kp", "vp", "lfp", "srp", "sip", "cvp", "ks", "vs", "lfs", "srs", "sis", "cvs")}

    for layer in range(depth):
        gn = row(norm_ffn[layer])
        wg, wu, wd = (w_gate[layer].astype(BF16), w_up[layer].astype(BF16), w_down[layer].astype(BF16))
        if layer % 2 == 0:
            e = layer // 2
            w = w_in[e]
            wuq = w[:, :d_a + d_b].astype(BF16)
            wkvT = w[:, d_a + d_b:d_a + 3 * d_b].T.astype(BF16)
            wfT = jnp.zeros((16, dm), F32).at[:n_heads].set(w[:, d_a + 3 * d_b:].T).astype(BF16)
            proj = (row(norm_mix[layer]), wuq, wkvT, wfT, b_f[e].reshape(n_heads, 1),
                    row(jnp.tile(q_norm[e], n_heads) * hd ** -0.5), k_norm[e].reshape(hd, 1), seg, n_heads)
            s5w = _s5_weights(lam_re[e], lam_im[e], log_step[e], b_re[e], b_im[e], c_re[e], c_im[e], d_skip[e])
            glu = (w_glu[e].astype(BF16), row(b_glu[e]))
            wo = w_out[e].astype(BF16)

            u, q, kT, vT, kTb, vTb, lfT = _in_proj(yp, nb, tm_p, *proj)
            zeros = jnp.zeros((nb, n_state), F32)
            s_out, hr, hi = _s5(u.reshape(length * nb, d_a), zeros, zeros, *s5w, *glu, bsz=nb, tl=tl_s5)
            dc = _cumsum_rows(lfT.reshape(nb * n_heads, length)).reshape(nb, n_heads, length)
            a_out = _attn_prompt(q, kTb, vTb, dc, blk_attn)
            yp = _mix_ffn(yp, nb, tm_p, [s_out.reshape(length, nb * d_a), a_out], wo, None,
                          gn, wg, wu, wd)
            outs["kp"].append(jnp.transpose(kT.reshape(nb, n_heads, hd, length), (0, 3, 1, 2)))
            outs["vp"].append(jnp.transpose(vT.reshape(nb, n_heads, hd, length), (0, 3, 1, 2)))
            outs["lfp"].append(jnp.transpose(lfT, (0, 2, 1)))
            outs["srp"].append(hr.reshape(nb, -1, state_ssm_re.shape[3]))
            outs["sip"].append(hi.reshape(nb, -1, state_ssm_re.shape[3]))

            u, q, kT, vT, _, _, lfT = _in_proj(ys, 1, db, *proj)
            k_new, v_new, lf_new = kT[0].T, vT[0].T, lfT[0].T
            s_out, hr, hi = _s5(u, state_ssm_re[e].reshape(db, n_state), state_ssm_im[e].reshape(db, n_state),
                                *s5w, *glu, bsz=db, tl=1)
            a_out = _attn_decode(q.astype(F32), k_new, v_new, lf_new, ck_t, cv_t, clf_t, page_table, e,
                                 pages_per_step)
            ys = _mix_ffn(ys, 1, db, [s_out, a_out], wo, None, gn, wg, wu, wd)
            outs["ks"].append(k_new.reshape(db, 1, n_heads, hd))
            outs["vs"].append(v_new.reshape(db, 1, n_heads, hd))
            outs["lfs"].append(lf_new.reshape(db, 1, n_heads))
            outs["srs"].append(hr.reshape(db, -1, state_ssm_re.shape[3]))
            outs["sis"].append(hi.reshape(db, -1, state_ssm_re.shape[3]))
        else:
            o = layer // 2
            gm = row(norm_mix[layer])
            w1, b1 = w_pw1[o].astype(BF16), row(b_pw1[o])
            w_pad = jnp.zeros((-(-width // 8) * 8, dm), F32).at[:width].set(w_dw[o])
            dw = (w_pad, row(b_dw[o]), row(ln_g[o]), row(ln_b[o]))
            w2, b2 = w_pw2[o].astype(BF16), row(b_pw2[o])

            g = _pw1(yp, tm_p, gm, w1, b1)
            act = _conv_prompt(g, nb, tl_conv, *dw, width=width)
            yp = _mix_ffn(yp, nb, tm_p, [act], w2, b2, gn, wg, wu, wd)
            outs["cvp"].append(g.reshape(nb, length, dm)[:, length - (width - 1):])

            g = _pw1(ys, db, gm, w1, b1)
            hist = state_conv[o]
            act = _conv_sample(jnp.transpose(hist, (1, 0, 2)), g, *dw, width=width)
            ys = _mix_ffn(ys, 1, db, [act], w2, b2, gn, wg, wu, wd)
            outs["cvs"].append(jnp.concatenate([hist[:, 1:], g[:, None, :]], axis=1))

    st = lambda k: jnp.stack(outs[k])
    return (yp.reshape(nb, length, dm), ys.reshape(db, 1, dm),
            st("kp"), st("vp"), st("lfp"), st("srp"), st("sip"), st("cvp"),
            st("ks"), st("vs"), st("lfs"), st("srs"), st("sis"), st("cvs"))
```

```python
import functools
import math

import numpy as np
import jax
import jax.numpy as jnp
from jax import lax
from jax.experimental import pallas as pl
from jax.experimental.pallas import tpu as pltpu

F32 = jnp.float32
BF16 = jnp.bfloat16
EPS = 1e-6
MASKED = -1e30
LANES = 128
VMEM_LIMIT = 56 * 1024 * 1024
SQRT_2_OVER_PI = float(np.float32(math.sqrt(2.0 / math.pi)))
LOG2E = math.log2(math.e)


def _dot(a, b):
    return jnp.dot(a, b, preferred_element_type=F32)


def _dot_nt(a, b):
    return lax.dot_general(a, b, (((1,), (1,)), ((), ())), preferred_element_type=F32)


def _sigmoid(x):
    return 1.0 / (1.0 + jnp.exp(-x))


def _rms_rows(x, g):
    return x * lax.rsqrt(jnp.mean(x * x, axis=-1, keepdims=True) + EPS) * g


def _params(*sem):
    return pltpu.CompilerParams(dimension_semantics=sem, vmem_limit_bytes=VMEM_LIMIT)


def _resident(shape):
    zeros = (0,) * len(shape)
    return pl.BlockSpec(shape, lambda *_: zeros, pipeline_mode=pl.Buffered(1))


def _inproj_kernel(x_ref, g_ref, wuq_ref, wkvT_ref, wfT_ref, bf_ref, qn_ref, kn_ref, seg_ref,
                   u_ref, q_ref, kT_ref, vT_ref, kTb_ref, vTb_ref, lfT_ref, *, n_heads):
    h = _rms_rows(x_ref[...], g_ref[...]).astype(BF16)
    z = _dot(h, wuq_ref[...])
    d = q_ref.shape[1]
    u_ref[...] = z[:, :z.shape[1] - d]
    q = z[:, z.shape[1] - d:]
    q_ms = _dot((q * q).astype(BF16), seg_ref[...])
    q_ref[...] = (q * lax.rsqrt(q_ms + EPS) * qn_ref[...]).astype(BF16)
    zT = _dot_nt(wkvT_ref[...], h)
    tm = zT.shape[1]
    k3 = zT[:d].reshape(n_heads, d // n_heads, tm)
    k3 = k3 * lax.rsqrt(jnp.mean(k3 * k3, axis=1, keepdims=True) + EPS) * kn_ref[...][None]
    kT = k3.reshape(d, tm)
    kT_ref[...] = kT
    kTb_ref[...] = kT.astype(BF16)
    vT = zT[d:]
    vT_ref[...] = vT
    vTb_ref[...] = vT.astype(BF16)
    f = _dot_nt(wfT_ref[...], h)[:n_heads] + bf_ref[...]
    lfT_ref[...] = jnp.minimum(f, 0.0) - jnp.log1p(jnp.exp(-jnp.abs(f)))


def _in_proj(x2d, nb, tm, g, wuq, wkvT, wfT, bf_col, qn_row, kn_col, seg, n_heads):
    rows, dm = x2d.shape
    lr = rows // nb
    nt = lr // tm
    d = qn_row.shape[1]
    da = wuq.shape[1] - d
    out_shape = (
        jax.ShapeDtypeStruct((lr, nb * da), F32),
        jax.ShapeDtypeStruct((rows, d), BF16),
        jax.ShapeDtypeStruct((nb, d, lr), F32),
        jax.ShapeDtypeStruct((nb, d, lr), F32),
        jax.ShapeDtypeStruct((nb, d, lr), BF16),
        jax.ShapeDtypeStruct((nb, d, lr), BF16),
        jax.ShapeDtypeStruct((nb, n_heads, lr), F32),
    )
    t_spec = pl.BlockSpec((None, d, tm), lambda b, t: (b, 0, t))
    return pl.pallas_call(
        functools.partial(_inproj_kernel, n_heads=n_heads),
        grid=(nb, nt),
        in_specs=[
            pl.BlockSpec((tm, dm), lambda b, t: (b * nt + t, 0)),
            _resident(g.shape), _resident(wuq.shape), _resident(wkvT.shape), _resident(wfT.shape),
            _resident(bf_col.shape), _resident(qn_row.shape), _resident(kn_col.shape), _resident(seg.shape),
        ],
        out_specs=(
            pl.BlockSpec((tm, da), lambda b, t: (t, b)),
            pl.BlockSpec((tm, d), lambda b, t: (b * nt + t, 0)),
            t_spec, t_spec, t_spec, t_spec,
            pl.BlockSpec((None, n_heads, tm), lambda b, t: (b, 0, t)),
        ),
        out_shape=out_shape,
        compiler_params=_params("parallel", "parallel"),
        name="in_proj",
    )(x2d, g, wuq, wkvT, wfT, bf_col, qn_row, kn_col, seg)


def _s5_kernel(u_ref, h0r_ref, h0i_ref, wb_ref, lbr_ref, lbi_ref, wc_ref, dsk_ref, wglu_ref, bglu_ref,
               s_ref, hr_out, hi_out, hbuf, hst_r, hst_i, *, bsz, tl):
    nblk, cb, sc2 = wb_ref.shape
    sc = sc2 // 2

    @pl.when(pl.program_id(0) == 0)
    def _():
        hst_r[...] = h0r_ref[...]
        hst_i[...] = h0i_ref[...]

    u = u_ref[...]
    for j in range(nblk):
        hbuf[:, sc2 * j:sc2 * (j + 1)] = _dot(u[:, cb * j:cb * (j + 1)].astype(BF16), wb_ref[j])

    for j in range(nblk):
        lre = jnp.broadcast_to(lbr_ref[:, sc * j:sc * (j + 1)], (bsz, sc))
        lim = jnp.broadcast_to(lbi_ref[:, sc * j:sc * (j + 1)], (bsz, sc))
        c_re, c_im = sc2 * j, sc2 * j + sc

        def step(t, carry, lre=lre, lim=lim, c_re=c_re, c_im=c_im):
            hr, hi = carry
            r0 = pl.multiple_of(t * bsz, bsz)
            nr = lre * hr - lim * hi + hbuf[pl.ds(r0, bsz), c_re:c_re + sc]
            ni = lre * hi + lim * hr + hbuf[pl.ds(r0, bsz), c_im:c_im + sc]
            hbuf[pl.ds(r0, bsz), c_re:c_re + sc] = nr
            hbuf[pl.ds(r0, bsz), c_im:c_im + sc] = ni
            return nr, ni

        hr, hi = lax.fori_loop(0, tl, step, (hst_r[:, sc * j:sc * (j + 1)], hst_i[:, sc * j:sc * (j + 1)]),
                               unroll=min(tl, 4))
        hst_r[:, sc * j:sc * (j + 1)] = hr
        hst_i[:, sc * j:sc * (j + 1)] = hi

    ys = []
    for j in range(nblk):
        hj = hbuf[:, sc2 * j:sc2 * (j + 1)].astype(BF16)
        ys.append(_dot(hj, wc_ref[j]) + dsk_ref[:, cb * j:cb * (j + 1)] * u[:, cb * j:cb * (j + 1)])
    y = jnp.concatenate(ys, axis=1)
    y = 0.5 * y * (1.0 + jnp.tanh(SQRT_2_OVER_PI * (y + 0.044715 * (y * y * y))))
    z = _dot(y.astype(BF16), wglu_ref[...]) + bglu_ref[...]
    s_ref[...] = (y * _sigmoid(z)).astype(BF16)
    hr_out[...] = hst_r[...]
    hi_out[...] = hst_i[...]


def _s5(u_tm, h0r, h0i, wb, lbr, lbi, wc, dsk, wglu, bglu, bsz, tl):
    rows, da = u_tm.shape
    nstate = h0r.shape[1]
    steps = rows // (tl * bsz)
    blk = tl * bsz
    return pl.pallas_call(
        functools.partial(_s5_kernel, bsz=bsz, tl=tl),
        grid=(steps,),
        in_specs=[
            pl.BlockSpec((blk, da), lambda t: (t, 0)),
            _resident(h0r.shape), _resident(h0i.shape), _resident(wb.shape), _resident(lbr.shape),
            _resident(lbi.shape), _resident(wc.shape), _resident(dsk.shape), _resident(wglu.shape),
            _resident(bglu.shape),
        ],
        out_specs=(
            pl.BlockSpec((blk, da), lambda t: (t, 0)),
            pl.BlockSpec(h0r.shape, lambda t: (0, 0)),
            pl.BlockSpec(h0i.shape, lambda t: (0, 0)),
        ),
        out_shape=(
            jax.ShapeDtypeStruct((rows, da), BF16),
            jax.ShapeDtypeStruct(h0r.shape, F32),
            jax.ShapeDtypeStruct(h0i.shape, F32),
        ),
        scratch_shapes=[
            pltpu.VMEM((blk, 2 * nstate), F32),
            pltpu.VMEM(h0r.shape, F32),
            pltpu.VMEM(h0i.shape, F32),
        ],
        compiler_params=_params("arbitrary"),
        name="s5_mixer",
    )(u_tm, h0r, h0i, wb, lbr, lbi, wc, dsk, wglu, bglu)


def _s5_weights(lam_re, lam_im, log_step, b_re, b_im, c_re, c_im, d_skip):
    n_groups, n_state = lam_re.shape
    p = b_re.shape[-1]
    gpb = LANES // p
    nblk = n_groups // gpb
    step = jnp.exp(log_step)[:, None]
    mag = jnp.exp(lam_re * step)
    lbr = mag * jnp.cos(lam_im * step)
    lbi = mag * jnp.sin(lam_im * step)
    den = lam_re * lam_re + lam_im * lam_im
    fr = ((lbr - 1.0) * lam_re + lbi * lam_im) / den
    fi = (lbi * lam_re - (lbr - 1.0) * lam_im) / den
    bbr = fr[..., None] * b_re - fi[..., None] * b_im
    bbi = fr[..., None] * b_im + fi[..., None] * b_re
    eye = jnp.eye(gpb, dtype=F32)

    def b_block(bb):
        t = bb.reshape(nblk, gpb, n_state, p)
        return jnp.einsum('jgnp,gh->jgphn', t, eye).reshape(nblk, gpb * p, gpb * n_state)

    def c_block(cc):
        t = cc.reshape(nblk, gpb, p, n_state)
        return jnp.einsum('jgpn,gh->jgnhp', t, eye).reshape(nblk, gpb * n_state, gpb * p)

    wb = jnp.concatenate([b_block(bbr), b_block(bbi)], axis=-1).astype(BF16)
    wc = jnp.concatenate([c_block(c_re), -c_block(c_im)], axis=1).astype(BF16)
    return (wb, lbr.reshape(1, n_groups * n_state), lbi.reshape(1, n_groups * n_state), wc,
            d_skip.reshape(1, n_groups * p))


def _cumsum_kernel(x_ref, o_ref):
    rows, length = x_ref.shape
    lane = lax.broadcasted_iota(jnp.int32, (rows, LANES), 1)
    carry = jnp.zeros((rows, 1), F32)
    for c in range(length // LANES):
        x = x_ref[:, LANES * c:LANES * (c + 1)]
        d = 1
        while d < LANES:
            x = x + jnp.where(lane >= d, pltpu.roll(x, d, 1), 0.0)
            d *= 2
        x = x + carry
        o_ref[:, LANES * c:LANES * (c + 1)] = x
        carry = x[:, LANES - 1:LANES]


def _cumsum_rows(x):
    return pl.pallas_call(
        _cumsum_kernel,
        out_shape=jax.ShapeDtypeStruct(x.shape, F32),
        name="logf_cumsum",
    )(x)


def _split3_bf16(x):
    hi = x.astype(BF16)
    r = x - hi.astype(F32)
    mid = r.astype(BF16)
    return hi, mid, (r - mid.astype(F32)).astype(BF16)


def _attn_kernel(q_ref, kT_ref, vT_ref, dc_ref, o_ref, kx_ref, vx_ref, *, blk, nsub, hd):
    hp = pl.program_id(1)
    qi = pl.program_id(2)
    d2 = 2 * hd
    n_bias = 16
    length = kT_ref.shape[1]

    @pl.when(qi == 0)
    def _():
        kx_ref[0:d2, :] = kT_ref[...]
        pieces = []
        for hh in range(2):
            pieces += list(_split3_bf16(-LOG2E * dc_ref[pl.ds(2 * hp + hh, 1), :]))
        pieces.append(jnp.zeros((n_bias - len(pieces), length), BF16))
        kx_ref[d2:d2 + n_bias, :] = jnp.concatenate(pieces, axis=0)
        kx_ref[d2 + n_bias:, :] = jnp.zeros((kx_ref.shape[0] - d2 - n_bias, length), BF16)
        sub = lax.broadcasted_iota(jnp.int32, (d2, 1), 0)
        vt = vT_ref[...]
        for hh in range(2):
            own = (sub >= hd) if hh else (sub < hd)
            vx_ref[hh] = jnp.where(own, vt, jnp.ones_like(vt))

    lane = lax.broadcasted_iota(jnp.int32, (1, d2), 1)
    row = lax.broadcasted_iota(jnp.int32, (blk, blk), 0)
    col = lax.broadcasted_iota(jnp.int32, (blk, blk), 1)
    qx = []
    for sb in range(nsub):
        q = q_ref[blk * sb:blk * (sb + 1), :]
        for hh in range(2):
            in_head = (lane >= hd) if hh else (lane < hd)
            sel = (lane >= 3 * hh) & (lane < 3 * hh + 3)
            qx.append(jnp.concatenate(
                [jnp.where(in_head, q, jnp.zeros_like(q)),
                 jnp.broadcast_to(jnp.where(sel, 1.0, 0.0).astype(BF16), (blk, d2))], axis=1))

    def step(kb, carry, first_sub, diag):
        ks = pl.multiple_of(kb * blk, blk)
        kblk = kx_ref[:, pl.ds(ks, blk)]
        out = list(carry)
        for ch in range(2 * first_sub, 2 * nsub):
            m, acc = carry[ch]
            s = _dot(qx[ch], kblk)
            if diag and ch // 2 == first_sub:
                s = jnp.where(col <= row, s, MASKED)
            m_new = jnp.maximum(m, jnp.max(s, axis=-1, keepdims=True))
            p = jnp.exp2(s - m_new).astype(BF16)
            acc = jnp.exp2(m - m_new) * acc + _dot_nt(p, vx_ref[ch % 2, :, pl.ds(ks, blk)])
            out[ch] = (m_new, acc)
        return tuple(out)

    init = tuple((jnp.full((blk, 1), MASKED, F32), jnp.zeros((blk, d2), F32)) for _ in range(2 * nsub))
    carry = lax.fori_loop(0, nsub * qi, functools.partial(step, first_sub=0, diag=False), init)
    for sb in range(nsub):
        carry = step(nsub * qi + sb, carry, sb, True)
    for sb in range(nsub):
        acc0, acc1 = carry[2 * sb][1], carry[2 * sb + 1][1]
        o_ref[blk * sb:blk * (sb + 1), :] = jnp.where(
            lane < hd, acc0 / pltpu.roll(acc0, hd, 1), acc1 / pltpu.roll(acc1, hd, 1)).astype(o_ref.dtype)


def _attn_prompt(q, kTb, vTb, dc, blk, nsub):
    nb, d, length = kTb.shape
    n_heads = dc.shape[1]
    hd = d // n_heads
    nq = length // (blk * nsub)
    return pl.pallas_call(
        functools.partial(_attn_kernel, blk=blk, nsub=nsub, hd=hd),
        grid=(nb, n_heads // 2, nq),
        in_specs=[
            pl.BlockSpec((blk * nsub, 2 * hd), lambda b, hp, qi: (b * nq + qi, hp)),
            pl.BlockSpec((None, 2 * hd, length), lambda b, hp, qi: (b, hp, 0)),
            pl.BlockSpec((None, 2 * hd, length), lambda b, hp, qi: (b, hp, 0)),
            pl.BlockSpec((None, n_heads, length), lambda b, hp, qi: (b, 0, 0)),
        ],
        out_specs=pl.BlockSpec((blk * nsub, 2 * hd), lambda b, hp, qi: (b * nq + qi, hp)),
        out_shape=jax.ShapeDtypeStruct(q.shape, BF16),
        scratch_shapes=[pltpu.VMEM((4 * hd, length), BF16), pltpu.VMEM((2, 2 * hd, length), BF16)],
        compiler_params=_params("parallel", "parallel", "arbitrary"),
        name="fox_prompt",
    )(q, kTb, vTb, dc)


def _decode_kernel(pt_ref, q_ref, kn_ref, vn_ref, lfn_ref, *refs, pages_per_step, hd):
    del pt_ref
    g = pages_per_step
    k_refs, v_refs, lf_refs = refs[:g], refs[g:2 * g], refs[2 * g:3 * g]
    o_ref, q8_ref, m_ref, l_ref, acc_ref, suf_ref = refs[3 * g:]
    n_heads, d = q8_ref.shape
    s_idx = pl.program_id(1)
    head_of_lane = lax.broadcasted_iota(jnp.int32, (n_heads, d), 1) // hd
    head_of_row = lax.broadcasted_iota(jnp.int32, (n_heads, d), 0)
    own = head_of_lane == head_of_row

    @pl.when(s_idx == 0)
    def _():
        q8 = jnp.where(own, jnp.broadcast_to(q_ref[0], (n_heads, d)), 0.0)
        q8_ref[...] = q8.astype(BF16)
        m_ref[...] = jnp.sum(q8 * kn_ref[0], axis=-1, keepdims=True)
        l_ref[...] = jnp.ones(l_ref.shape, F32)
        acc_ref[...] = jnp.broadcast_to(vn_ref[0], (n_heads, d))
        suf_ref[...] = lfn_ref[0]

    lane = lax.broadcasted_iota(jnp.int32, (n_heads, LANES), 1)
    q8 = q8_ref[...]
    m, l, acc, suf = m_ref[...], l_ref[...], acc_ref[...], suf_ref[...]
    logits = []
    for i in range(g):
        lf = lf_refs[i][0, 0]
        x = lf
        sh = 1
        while sh < LANES:
            x = x + jnp.where(lane < LANES - sh, pltpu.roll(x, LANES - sh, 1), 0.0)
            sh *= 2
        bias = (x - lf) + suf
        suf = suf + x[:, 0:1]
        kt = k_refs[i][0, 0].reshape(d, LANES).astype(BF16)
        logits.append(_dot(q8, kt) + LOG2E * bias)
    s = jnp.concatenate(logits, axis=1)
    m_new = jnp.maximum(m, jnp.max(s, axis=-1, keepdims=True))
    alpha = jnp.exp2(m - m_new)
    p = jnp.exp2(s - m_new)
    l = alpha * l + jnp.sum(p, axis=-1, keepdims=True)
    p = p.astype(BF16)
    pv = None
    for i in range(g):
        vt = v_refs[i][0, 0].reshape(d, LANES).astype(BF16)
        term = _dot_nt(p[:, LANES * i:LANES * (i + 1)], vt)
        pv = term if pv is None else pv + term
    acc = alpha * acc + pv
    m_ref[...], l_ref[...], acc_ref[...], suf_ref[...] = m_new, l, acc, suf

    @pl.when(s_idx == pl.num_programs(1) - 1)
    def _():
        o_ref[0] = jnp.sum(jnp.where(own, acc / l, 0.0), axis=0, keepdims=True).astype(o_ref.dtype)


def _attn_decode(q, k_new, v_new, lf_new, ck_t, cv_t, clf_t, page_table, layer, pages_per_step):
    db, d = q.shape
    n_heads, hd, page = ck_t.shape[2:]
    n_pages = page_table.shape[1]
    g = pages_per_step
    steps = n_pages // g

    def page_of(b, s, pt, i):
        return pt[b, n_pages - 1 - (s * g + i)]

    kv_specs = [pl.BlockSpec((1, 1, n_heads, hd, page),
                             lambda b, s, pt, i=i: (layer, page_of(b, s, pt, i), 0, 0, 0)) for i in range(g)]
    lf_specs = [pl.BlockSpec((1, 1, n_heads, page),
                             lambda b, s, pt, i=i: (layer, page_of(b, s, pt, i), 0, 0)) for i in range(g)]
    row_spec = pl.BlockSpec((1, 1, d), lambda b, s, pt: (b, 0, 0))
    grid_spec = pltpu.PrefetchScalarGridSpec(
        num_scalar_prefetch=1,
        grid=(db, steps),
        in_specs=[row_spec, row_spec, row_spec, pl.BlockSpec((1, n_heads, 1), lambda b, s, pt: (b, 0, 0))]
                 + kv_specs + kv_specs + lf_specs,
        out_specs=row_spec,
        scratch_shapes=[
            pltpu.VMEM((n_heads, d), BF16),
            pltpu.VMEM((n_heads, 1), F32),
            pltpu.VMEM((n_heads, 1), F32),
            pltpu.VMEM((n_heads, d), F32),
            pltpu.VMEM((n_heads, 1), F32),
        ],
    )
    out = pl.pallas_call(
        functools.partial(_decode_kernel, pages_per_step=g, hd=hd),
        grid_spec=grid_spec,
        out_shape=jax.ShapeDtypeStruct((db, 1, d), BF16),
        compiler_params=_params("parallel", "arbitrary"),
        name="fox_decode",
    )(page_table, q.reshape(db, 1, d), k_new.reshape(db, 1, d), v_new.reshape(db, 1, d),
      lf_new.reshape(db, n_heads, 1), *([ck_t] * g), *([cv_t] * g), *([clf_t] * g))
    return out.reshape(db, d)


def _pw1_kernel(x_ref, g_ref, w_ref, b_ref, o_ref):
    h = _rms_rows(x_ref[...], g_ref[...]).astype(BF16)
    z = _dot(h, w_ref[...]) + b_ref[...]
    d = o_ref.shape[1]
    o_ref[...] = z[:, :d] * _sigmoid(z[:, d:])


def _pw1(x2d, tm, g, w, b):
    rows, dm = x2d.shape
    dc = w.shape[1] // 2
    return pl.pallas_call(
        _pw1_kernel,
        grid=(rows // tm,),
        in_specs=[pl.BlockSpec((tm, dm), lambda i: (i, 0)), _resident(g.shape), _resident(w.shape),
                  _resident(b.shape)],
        out_specs=pl.BlockSpec((tm, dc), lambda i: (i, 0)),
        out_shape=jax.ShapeDtypeStruct((rows, dc), F32),
        compiler_params=_params("parallel"),
        name="conv_pw1_glu",
    )(x2d, g, w, b)


def _ln_swish(y, g, b):
    mu = jnp.mean(y, axis=-1, keepdims=True)
    yc = y - mu
    y = yc * lax.rsqrt(jnp.mean(yc * yc, axis=-1, keepdims=True) + EPS) * g + b
    return y * _sigmoid(y)


def _conv_kernel(g_ref, w_ref, bdw_ref, lng_ref, lnb_ref, o_ref, buf, ybuf, *, tl, rg, width):
    t = pl.program_id(0)
    bsz, _, dch = g_ref.shape
    hist = width - 1
    nch = dch // LANES
    jg = min(4, rg)

    @pl.when(t == 0)
    def _():
        buf[:, 0:hist * bsz, :] = jnp.zeros((nch, hist * bsz, LANES), F32)

    @pl.when(t > 0)
    def _():
        buf[:, 0:hist * bsz, :] = buf[:, tl * bsz:(tl + hist) * bsz, :]

    for b in range(bsz):
        for c in range(nch):
            buf[c, pl.ds(hist * bsz + b, tl, stride=bsz), :] = g_ref[b, :, LANES * c:LANES * (c + 1)]

    def chunk(ci, carry):
        r0 = pl.multiple_of(ci * (rg * bsz), rg * bsz)

        def lane_chunk(c, carry2):
            bias = jnp.broadcast_to(bdw_ref[c], (bsz, LANES))
            for j0 in range(0, rg, jg):
                accs = [bias] * jg
                for k in range(width):
                    tap = jnp.broadcast_to(w_ref[c, k:k + 1, :], (bsz, LANES))
                    for j in range(jg):
                        accs[j] = accs[j] + tap * buf[c, pl.ds(r0 + (j0 + j + k) * bsz, bsz), :]
                for j in range(jg):
                    ybuf[c, pl.ds(r0 + (j0 + j) * bsz, bsz), :] = accs[j]
            return carry2

        lax.fori_loop(0, nch, lane_chunk, 0)
        rows = pl.ds(r0, rg * bsz)
        y = ybuf[:, rows, :]
        mu = jnp.sum(jnp.sum(y, axis=0), axis=-1, keepdims=True) / dch
        yc = y - mu[None]
        var = jnp.sum(jnp.sum(yc * yc, axis=0), axis=-1, keepdims=True) / dch
        y = yc * lax.rsqrt(var + EPS)[None] * lng_ref[...] + lnb_ref[...]
        ybuf[:, rows, :] = y * _sigmoid(y)
        return carry

    lax.fori_loop(0, tl // rg, chunk, 0)
    for b in range(bsz):
        for c in range(nch):
            o_ref[b, :, LANES * c:LANES * (c + 1)] = ybuf[c, pl.ds(b, tl, stride=bsz), :].astype(o_ref.dtype)


def _conv_prompt(g3d, tl, w_pad, bdw, lng, lnb, width):
    nb, length, dch = g3d.shape
    hist = width - 1
    nch = dch // LANES
    lng, lnb, bdw = (v.reshape(nch, 1, LANES) for v in (lng, lnb, bdw))
    w_pad = jnp.transpose(w_pad.reshape(-1, nch, LANES), (1, 0, 2))
    return pl.pallas_call(
        functools.partial(_conv_kernel, tl=tl, rg=min(8, tl), width=width),
        grid=(length // tl,),
        in_specs=[pl.BlockSpec((nb, tl, dch), lambda t: (0, t, 0)), _resident(w_pad.shape),
                  _resident(bdw.shape), _resident(lng.shape), _resident(lnb.shape)],
        out_specs=pl.BlockSpec((nb, tl, dch), lambda t: (0, t, 0)),
        out_shape=jax.ShapeDtypeStruct((nb, length, dch), BF16),
        scratch_shapes=[pltpu.VMEM((nch, (hist + tl) * nb, LANES), F32), pltpu.VMEM((nch, tl * nb, LANES), F32)],
        compiler_params=_params("arbitrary"),
        name="conv_dw_prompt",
    )(g3d, w_pad, bdw, lng, lnb)


def _conv_step_kernel(hist_ref, g_ref, w_ref, bdw_ref, lng_ref, lnb_ref, o_ref, *, width):
    acc = bdw_ref[...] + w_ref[width - 1:width, :] * g_ref[...]
    for k in range(width - 1):
        acc = acc + w_ref[k:k + 1, :] * hist_ref[k]
    o_ref[...] = _ln_swish(acc, lng_ref[...], lnb_ref[...]).astype(o_ref.dtype)


def _conv_sample(hist_t, g2d, w_pad, bdw, lng, lnb, width):
    return pl.pallas_call(
        functools.partial(_conv_step_kernel, width=width),
        out_shape=jax.ShapeDtypeStruct(g2d.shape, BF16),
        compiler_params=pltpu.CompilerParams(vmem_limit_bytes=VMEM_LIMIT),
        name="conv_dw_sample",
    )(hist_t, g2d, w_pad, bdw, lng, lnb)


def _mixffn_kernel(*refs, n_mix, has_bias, ff_chunks):
    x_ref = refs[0]
    acts = refs[1:1 + n_mix]
    wo_ref = refs[1 + n_mix]
    pos = 2 + n_mix
    bias_ref = refs[pos] if has_bias else None
    pos += int(has_bias)
    gn_ref, wg_ref, wu_ref, wd_ref, o_ref = refs[pos:pos + 5]
    mixed = acts[0][...] if n_mix == 1 else jnp.concatenate([a[...] for a in acts], axis=1)
    x1 = x_ref[...] + _dot(mixed, wo_ref[...])
    if has_bias:
        x1 = x1 + bias_ref[...]
    h = _rms_rows(x1, gn_ref[...]).astype(BF16)
    acc = x1
    for c0, c1 in ff_chunks:
        gate = _dot(h, wg_ref[:, c0:c1])
        up = _dot(h, wu_ref[:, c0:c1])
        acc = acc + _dot((gate * _sigmoid(gate) * up).astype(BF16), wd_ref[c0:c1, :])
    o_ref[...] = acc


def _ff_chunks(dff, width=1024):
    return tuple((c, min(c + width, dff)) for c in range(0, dff, width))


def _mix_ffn(x2d, nb, tm, mix_acts, wo, bias, gn, wg, wu, wd):
    rows, dm = x2d.shape
    lr = rows // nb
    nt = lr // tm
    in_specs = [pl.BlockSpec((tm, dm), lambda b, t: (b * nt + t, 0))]
    for a in mix_acts:
        if a.shape[0] == rows:
            in_specs.append(pl.BlockSpec((tm, a.shape[1]), lambda b, t: (b * nt + t, 0)))
        else:
            in_specs.append(pl.BlockSpec((tm, a.shape[1] // nb), lambda b, t: (t, b)))
    consts = [wo] + ([bias] if bias is not None else []) + [gn, wg, wu, wd]
    in_specs += [_resident(c.shape) for c in consts]
    return pl.pallas_call(
        functools.partial(_mixffn_kernel, n_mix=len(mix_acts), has_bias=bias is not None,
                          ff_chunks=_ff_chunks(wg.shape[1])),
        grid=(nb, nt),
        in_specs=in_specs,
        out_specs=pl.BlockSpec((tm, dm), lambda b, t: (b * nt + t, 0)),
        out_shape=jax.ShapeDtypeStruct((rows, dm), F32),
        compiler_params=_params("parallel", "parallel"),
        name="mix_ffn",
    )(x2d, *mix_acts, *consts)


def kernel(x_prompt, x_sample, cache_k, cache_v, cache_logf, page_table, state_ssm_re, state_ssm_im, state_conv, norm_mix, norm_ffn, w_in, b_f, q_norm, k_norm, lam_re, lam_im, log_step, b_re, b_im, c_re, c_im, d_skip, w_glu, b_glu, w_out, w_pw1, b_pw1, w_dw, b_dw, ln_g, ln_b, w_pw2, b_pw2, w_gate, w_up, w_down):
    nb, length, dm = x_prompt.shape
    db = x_sample.shape[0]
    depth = norm_mix.shape[0]
    n_heads, hd = cache_k.shape[3], cache_k.shape[4]
    d_b = n_heads * hd
    d_a = w_in.shape[2] - 3 * d_b - n_heads
    n_state = state_ssm_re.shape[2] * state_ssm_re.shape[3]
    width = w_dw.shape[1]
    tm_p = min(512, length)
    blk_attn = min(256, length)
    nsub_attn = 2 if length % (2 * blk_attn) == 0 else 1
    tl_s5 = min(64, length)
    tl_conv = min(64, length)
    pages_per_step = min(16, page_table.shape[1])

    ck_t = jnp.transpose(cache_k, (0, 1, 3, 4, 2))
    cv_t = jnp.transpose(cache_v, (0, 1, 3, 4, 2))
    clf_t = jnp.transpose(cache_logf, (0, 1, 3, 2))

    seg = jnp.kron(jnp.eye(n_heads, dtype=F32), jnp.full((hd, hd), 1.0 / hd, F32)).astype(BF16)
    row = lambda v: v.reshape(1, -1).astype(F32)

    yp = x_prompt.reshape(nb * length, dm)
    ys = x_sample.reshape(db, dm)
    outs = {k: [] for k in ("kp", "vp", "lfp", "srp", "sip", "cvp", "ks", "vs", "lfs", "srs", "sis", "cvs")}

    for layer in range(depth):
        gn = row(norm_ffn[layer])
        wg, wu, wd = (w_gate[layer].astype(BF16), w_up[layer].astype(BF16), w_down[layer].astype(BF16))
        if layer % 2 == 0:
            e = layer // 2
            w = w_in[e]
            wuq = w[:, :d_a + d_b].astype(BF16)
            wkvT = w[:, d_a + d_b:d_a + 3 * d_b].T.astype(BF16)
            wfT = jnp.zeros((16, dm), F32).at[:n_heads].set(w[:, d_a + 3 * d_b:].T).astype(BF16)
            proj = (row(norm_mix[layer]), wuq, wkvT, wfT, b_f[e].reshape(n_heads, 1),
                    row(jnp.tile(q_norm[e], n_heads) * (hd ** -0.5 * LOG2E)), k_norm[e].reshape(hd, 1), seg, n_heads)
            s5w = _s5_weights(lam_re[e], lam_im[e], log_step[e], b_re[e], b_im[e], c_re[e], c_im[e], d_skip[e])
            glu = (w_glu[e].astype(BF16), row(b_glu[e]))
            wo = w_out[e].astype(BF16)

            u, q, kT, vT, kTb, vTb, lfT = _in_proj(yp, nb, tm_p, *proj)
            zeros = jnp.zeros((nb, n_state), F32)
            s_out, hr, hi = _s5(u.reshape(length * nb, d_a), zeros, zeros, *s5w, *glu, bsz=nb, tl=tl_s5)
            dc = _cumsum_rows(lfT.reshape(nb * n_heads, length)).reshape(nb, n_heads, length)
            a_out = _attn_prompt(q, kTb, vTb, dc, blk_attn, nsub_attn)
            yp = _mix_ffn(yp, nb, tm_p, [s_out.reshape(length, nb * d_a), a_out], wo, None,
                          gn, wg, wu, wd)
            outs["kp"].append(jnp.transpose(kT.reshape(nb, n_heads, hd, length), (0, 3, 1, 2)))
            outs["vp"].append(jnp.transpose(vT.reshape(nb, n_heads, hd, length), (0, 3, 1, 2)))
            outs["lfp"].append(jnp.transpose(lfT, (0, 2, 1)))
            outs["srp"].append(hr.reshape(nb, -1, state_ssm_re.shape[3]))
            outs["sip"].append(hi.reshape(nb, -1, state_ssm_re.shape[3]))

            u, q, kT, vT, _, _, lfT = _in_proj(ys, 1, db, *proj)
            k_new, v_new, lf_new = kT[0].T, vT[0].T, lfT[0].T
            s_out, hr, hi = _s5(u, state_ssm_re[e].reshape(db, n_state), state_ssm_im[e].reshape(db, n_state),
                                *s5w, *glu, bsz=db, tl=1)
            a_out = _attn_decode(q.astype(F32), k_new, v_new, lf_new, ck_t, cv_t, clf_t, page_table, e,
                                 pages_per_step)
            ys = _mix_ffn(ys, 1, db, [s_out, a_out], wo, None, gn, wg, wu, wd)
            outs["ks"].append(k_new.reshape(db, 1, n_heads, hd))
            outs["vs"].append(v_new.reshape(db, 1, n_heads, hd))
            outs["lfs"].append(lf_new.reshape(db, 1, n_heads))
            outs["srs"].append(hr.reshape(db, -1, state_ssm_re.shape[3]))
            outs["sis"].append(hi.reshape(db, -1, state_ssm_re.shape[3]))
        else:
            o = layer // 2
            gm = row(norm_mix[layer])
            w1, b1 = w_pw1[o].astype(BF16), row(b_pw1[o])
            w_pad = jnp.zeros((-(-width // 8) * 8, dm), F32).at[:width].set(w_dw[o])
            dw = (w_pad, row(b_dw[o]), row(ln_g[o]), row(ln_b[o]))
            w2, b2 = w_pw2[o].astype(BF16), row(b_pw2[o])

            g = _pw1(yp, tm_p, gm, w1, b1)
            act = _conv_prompt(g.reshape(nb, length, dm), tl_conv, *dw, width=width).reshape(nb * length, dm)
            yp = _mix_ffn(yp, nb, tm_p, [act], w2, b2, gn, wg, wu, wd)
            outs["cvp"].append(g.reshape(nb, length, dm)[:, length - (width - 1):])

            g = _pw1(ys, db, gm, w1, b1)
            hist = state_conv[o]
            act = _conv_sample(jnp.transpose(hist, (1, 0, 2)), g, *dw, width=width)
            ys = _mix_ffn(ys, 1, db, [act], w2, b2, gn, wg, wu, wd)
            outs["cvs"].append(jnp.concatenate([hist[:, 1:], g[:, None, :]], axis=1))

    st = lambda k: jnp.stack(outs[k])
    return (yp.reshape(nb, length, dm), ys.reshape(db, 1, dm),
            st("kp"), st("vp"), st("lfp"), st("srp"), st("sip"), st("cvp"),
            st("ks"), st("vs"), st("lfs"), st("srs"), st("sis"), st("cvs"))
```

```python
import functools
import math

import numpy as np
import jax
import jax.numpy as jnp
from jax import lax
from jax.experimental import pallas as pl
from jax.experimental.pallas import tpu as pltpu

F32 = jnp.float32
BF16 = jnp.bfloat16
EPS = 1e-6
MASKED = -1e30
LANES = 128
VMEM_LIMIT = 56 * 1024 * 1024
SQRT_2_OVER_PI = float(np.float32(math.sqrt(2.0 / math.pi)))
LOG2E = math.log2(math.e)


def _dot(a, b):
    return jnp.dot(a, b, preferred_element_type=F32)


def _dot_nt(a, b):
    return lax.dot_general(a, b, (((1,), (1,)), ((), ())), preferred_element_type=F32)


def _sigmoid(x):
    return 1.0 / (1.0 + jnp.exp(-x))


def _rms_rows(x, g):
    return x * lax.rsqrt(jnp.mean(x * x, axis=-1, keepdims=True) + EPS) * g


def _params(*sem):
    return pltpu.CompilerParams(dimension_semantics=sem, vmem_limit_bytes=VMEM_LIMIT)


def _resident(shape):
    zeros = (0,) * len(shape)
    return pl.BlockSpec(shape, lambda *_: zeros, pipeline_mode=pl.Buffered(1))


def _inproj_kernel(x_ref, g_ref, wuq_ref, wkvT_ref, wfT_ref, bf_ref, qn_ref, kn_ref, seg_ref, *refs, n_heads):
    u_ref, q_ref, kT_ref, vT_ref, kTb_ref, vTb_ref, lfT_ref = refs[-7:]
    h = _rms_rows(x_ref[...], g_ref[...]).astype(BF16)
    z = _dot(h, wuq_ref[...])
    d = q_ref.shape[1]
    u_ref[...] = z[:, :z.shape[1] - d]
    q = z[:, z.shape[1] - d:]
    q_ms = _dot((q * q).astype(BF16), seg_ref[...])
    q_ref[...] = (q * lax.rsqrt(q_ms + EPS) * qn_ref[...]).astype(BF16)
    zT = _dot_nt(wkvT_ref[...], h)
    tm = zT.shape[1]
    k3 = zT[:d].reshape(n_heads, d // n_heads, tm)
    k3 = k3 * lax.rsqrt(jnp.mean(k3 * k3, axis=1, keepdims=True) + EPS) * kn_ref[...][None]
    kT = k3.reshape(d, tm)
    kT_ref[...] = kT
    kTb_ref[...] = kT.astype(BF16)
    vT = zT[d:]
    vT_ref[...] = vT
    vTb_ref[...] = vT.astype(BF16)
    f = _dot_nt(wfT_ref[...], h)[:n_heads] + bf_ref[...]
    lfT_ref[...] = jnp.minimum(f, 0.0) - jnp.log1p(jnp.exp(-jnp.abs(f)))


def _in_proj(x2d, nb, tm, g, wuq, wkvT, wfT, bf_col, qn_row, kn_col, seg, n_heads, slot=0, n_slots=1, stacks=None):
    rows, dm = x2d.shape
    lr = rows // nb
    nt = lr // tm
    d = qn_row.shape[1]
    da = wuq.shape[1] - d
    out_shape = (
        jax.ShapeDtypeStruct((rows, da), F32),
        jax.ShapeDtypeStruct((rows, d), BF16),
        jax.ShapeDtypeStruct((n_slots, nb, d, lr), F32),
        jax.ShapeDtypeStruct((n_slots, nb, d, lr), F32),
        jax.ShapeDtypeStruct((nb, d, lr), BF16),
        jax.ShapeDtypeStruct((nb, d, lr), BF16),
        jax.ShapeDtypeStruct((nb, n_heads, lr), F32),
    )
    t_spec = pl.BlockSpec((None, d, tm), lambda b, t: (b, 0, t))
    slot_spec = pl.BlockSpec((None, None, d, tm), lambda b, t: (slot, b, 0, t))
    row_spec = lambda width: pl.BlockSpec((tm, width), lambda b, t: (b * nt + t, 0))
    consts = (g, wuq, wkvT, wfT, bf_col, qn_row, kn_col, seg)
    extra = () if stacks is None else tuple(stacks)
    return pl.pallas_call(
        functools.partial(_inproj_kernel, n_heads=n_heads),
        grid=(nb, nt),
        in_specs=[row_spec(dm)] + [_resident(c.shape) for c in consts]
                 + [pl.BlockSpec(memory_space=pl.ANY)] * len(extra),
        out_specs=(row_spec(da), row_spec(d), slot_spec, slot_spec, t_spec, t_spec,
                   pl.BlockSpec((None, n_heads, tm), lambda b, t: (b, 0, t))),
        out_shape=out_shape,
        input_output_aliases={1 + len(consts) + i: 2 + i for i in range(len(extra))},
        compiler_params=_params("parallel", "parallel"),
        name="in_proj",
    )(x2d, *consts, *extra)


def _s5_kernel(u_ref, h0r_ref, h0i_ref, wb_ref, lbr_ref, lbi_ref, wc_ref, dsk_ref, wglu_ref, bglu_ref,
               s_ref, hr_out, hi_out, hbuf, hst_r, hst_i, uil, sil, *, bsz, tl):
    nblk, cb, sc2 = wb_ref.shape
    sc = sc2 // 2
    interleave = len(u_ref.shape) == 3

    @pl.when(pl.program_id(0) == 0)
    def _():
        hst_r[...] = h0r_ref[...]
        hst_i[...] = h0i_ref[...]

    if interleave:
        for b in range(bsz):
            for j in range(nblk):
                uil[j, pl.ds(b, tl, stride=bsz), :] = u_ref[b, :, cb * j:cb * (j + 1)]
    else:
        for j in range(nblk):
            uil[j] = u_ref[:, cb * j:cb * (j + 1)]

    for j in range(nblk):
        hbuf[:, sc2 * j:sc2 * (j + 1)] = _dot(uil[j].astype(BF16), wb_ref[j])

    for j in range(nblk):
        lre = jnp.broadcast_to(lbr_ref[:, sc * j:sc * (j + 1)], (bsz, sc))
        lim = jnp.broadcast_to(lbi_ref[:, sc * j:sc * (j + 1)], (bsz, sc))
        c_re, c_im = sc2 * j, sc2 * j + sc

        def step(t, carry, lre=lre, lim=lim, c_re=c_re, c_im=c_im):
            hr, hi = carry
            r0 = pl.multiple_of(t * bsz, bsz)
            nr = lre * hr - lim * hi + hbuf[pl.ds(r0, bsz), c_re:c_re + sc]
            ni = lre * hi + lim * hr + hbuf[pl.ds(r0, bsz), c_im:c_im + sc]
            hbuf[pl.ds(r0, bsz), c_re:c_re + sc] = nr
            hbuf[pl.ds(r0, bsz), c_im:c_im + sc] = ni
            return nr, ni

        hr, hi = lax.fori_loop(0, tl, step, (hst_r[:, sc * j:sc * (j + 1)], hst_i[:, sc * j:sc * (j + 1)]),
                               unroll=min(tl, 4))
        hst_r[:, sc * j:sc * (j + 1)] = hr
        hst_i[:, sc * j:sc * (j + 1)] = hi

    ys = []
    for j in range(nblk):
        hj = hbuf[:, sc2 * j:sc2 * (j + 1)].astype(BF16)
        ys.append(_dot(hj, wc_ref[j]) + dsk_ref[:, cb * j:cb * (j + 1)] * uil[j])
    y = jnp.concatenate(ys, axis=1)
    y = 0.5 * y * (1.0 + jnp.tanh(SQRT_2_OVER_PI * (y + 0.044715 * (y * y * y))))
    z = _dot(y.astype(BF16), wglu_ref[...]) + bglu_ref[...]
    out = y * _sigmoid(z)
    if interleave:
        for j in range(nblk):
            sil[j] = out[:, cb * j:cb * (j + 1)]
        for b in range(bsz):
            for j in range(nblk):
                s_ref[b, :, cb * j:cb * (j + 1)] = sil[j, pl.ds(b, tl, stride=bsz), :].astype(s_ref.dtype)
    else:
        s_ref[...] = out.astype(s_ref.dtype)
    hr_out[...] = hst_r[...]
    hi_out[...] = hst_i[...]


def _s5(u, h0r, h0i, wb, lbr, lbi, wc, dsk, wglu, bglu, bsz, tl):
    nstate = h0r.shape[1]
    da = u.shape[-1]
    blk = tl * bsz
    if u.ndim == 3:
        steps = u.shape[1] // tl
        io_spec = pl.BlockSpec((bsz, tl, da), lambda t: (0, t, 0))
    else:
        steps = u.shape[0] // blk
        io_spec = pl.BlockSpec((blk, da), lambda t: (t, 0))
    nblk, cb = wb.shape[0], wb.shape[1]
    return pl.pallas_call(
        functools.partial(_s5_kernel, bsz=bsz, tl=tl),
        grid=(steps,),
        in_specs=[
            io_spec,
            _resident(h0r.shape), _resident(h0i.shape), _resident(wb.shape), _resident(lbr.shape),
            _resident(lbi.shape), _resident(wc.shape), _resident(dsk.shape), _resident(wglu.shape),
            _resident(bglu.shape),
        ],
        out_specs=(
            io_spec,
            pl.BlockSpec(h0r.shape, lambda t: (0, 0)),
            pl.BlockSpec(h0i.shape, lambda t: (0, 0)),
        ),
        out_shape=(
            jax.ShapeDtypeStruct(u.shape, BF16),
            jax.ShapeDtypeStruct(h0r.shape, F32),
            jax.ShapeDtypeStruct(h0i.shape, F32),
        ),
        scratch_shapes=[
            pltpu.VMEM((blk, 2 * nstate), F32),
            pltpu.VMEM(h0r.shape, F32),
            pltpu.VMEM(h0i.shape, F32),
            pltpu.VMEM((nblk, blk, cb), F32),
            pltpu.VMEM((nblk, blk, cb), F32),
        ],
        compiler_params=_params("arbitrary"),
        name="s5_mixer",
    )(u, h0r, h0i, wb, lbr, lbi, wc, dsk, wglu, bglu)


def _s5_weights(lam_re, lam_im, log_step, b_re, b_im, c_re, c_im, d_skip):
    n_groups, n_state = lam_re.shape
    p = b_re.shape[-1]
    gpb = LANES // p
    nblk = n_groups // gpb
    step = jnp.exp(log_step)[:, None]
    mag = jnp.exp(lam_re * step)
    lbr = mag * jnp.cos(lam_im * step)
    lbi = mag * jnp.sin(lam_im * step)
    den = lam_re * lam_re + lam_im * lam_im
    fr = ((lbr - 1.0) * lam_re + lbi * lam_im) / den
    fi = (lbi * lam_re - (lbr - 1.0) * lam_im) / den
    bbr = fr[..., None] * b_re - fi[..., None] * b_im
    bbi = fr[..., None] * b_im + fi[..., None] * b_re
    eye = jnp.eye(gpb, dtype=F32)

    def b_block(bb):
        t = bb.reshape(nblk, gpb, n_state, p)
        return jnp.einsum('jgnp,gh->jgphn', t, eye).reshape(nblk, gpb * p, gpb * n_state)

    def c_block(cc):
        t = cc.reshape(nblk, gpb, p, n_state)
        return jnp.einsum('jgpn,gh->jgnhp', t, eye).reshape(nblk, gpb * n_state, gpb * p)

    wb = jnp.concatenate([b_block(bbr), b_block(bbi)], axis=-1).astype(BF16)
    wc = jnp.concatenate([c_block(c_re), -c_block(c_im)], axis=1).astype(BF16)
    return (wb, lbr.reshape(1, n_groups * n_state), lbi.reshape(1, n_groups * n_state), wc,
            d_skip.reshape(1, n_groups * p))


def _cumsum_kernel(x_ref, o_ref):
    rows, length = x_ref.shape
    lane = lax.broadcasted_iota(jnp.int32, (rows, LANES), 1)
    carry = jnp.zeros((rows, 1), F32)
    for c in range(length // LANES):
        x = x_ref[:, LANES * c:LANES * (c + 1)]
        d = 1
        while d < LANES:
            x = x + jnp.where(lane >= d, pltpu.roll(x, d, 1), 0.0)
            d *= 2
        x = x + carry
        o_ref[:, LANES * c:LANES * (c + 1)] = x
        carry = x[:, LANES - 1:LANES]


def _cumsum_rows(x):
    return pl.pallas_call(
        _cumsum_kernel,
        out_shape=jax.ShapeDtypeStruct(x.shape, F32),
        name="logf_cumsum",
    )(x)


def _split3_bf16(x):
    hi = x.astype(BF16)
    r = x - hi.astype(F32)
    mid = r.astype(BF16)
    return hi, mid, (r - mid.astype(F32)).astype(BF16)


def _attn_kernel(q_ref, kT_ref, vT_ref, dc_ref, o_ref, kx_ref, vx_ref, m_ref, acc_ref, *, blk, nsub, hd):
    hp = pl.program_id(1)
    qi = pl.program_id(2)
    d2 = 2 * hd
    n_bias = 16
    length = kT_ref.shape[1]

    @pl.when(qi == 0)
    def _():
        kx_ref[0:d2, :] = kT_ref[...]
        pieces = []
        for hh in range(2):
            pieces += list(_split3_bf16(-LOG2E * dc_ref[pl.ds(2 * hp + hh, 1), :]))
        pieces.append(jnp.zeros((n_bias - len(pieces), length), BF16))
        kx_ref[d2:d2 + n_bias, :] = jnp.concatenate(pieces, axis=0)
        kx_ref[d2 + n_bias:, :] = jnp.zeros((kx_ref.shape[0] - d2 - n_bias, length), BF16)
        sub = lax.broadcasted_iota(jnp.int32, (d2, 1), 0)
        vt = vT_ref[...]
        for hh in range(2):
            own = (sub >= hd) if hh else (sub < hd)
            vx_ref[hh] = jnp.where(own, vt, jnp.ones_like(vt))

    lane = lax.broadcasted_iota(jnp.int32, (1, d2), 1)
    row = lax.broadcasted_iota(jnp.int32, (blk, blk), 0)
    col = lax.broadcasted_iota(jnp.int32, (blk, blk), 1)
    qx = []
    for sb in range(nsub):
        q = q_ref[blk * sb:blk * (sb + 1), :]
        for hh in range(2):
            in_head = (lane >= hd) if hh else (lane < hd)
            sel = (lane >= 3 * hh) & (lane < 3 * hh + 3)
            qx.append(jnp.concatenate(
                [jnp.where(in_head, q, jnp.zeros_like(q)),
                 jnp.broadcast_to(jnp.where(sel, 1.0, 0.0).astype(BF16), (blk, d2))], axis=1))

    m_ref[...] = jnp.full(m_ref.shape, MASKED, F32)
    acc_ref[...] = jnp.zeros(acc_ref.shape, F32)

    def step(kb, carry, first_sub, diag):
        ks = pl.multiple_of(kb * blk, blk)
        kblk = kx_ref[:, pl.ds(ks, blk)]
        for ch in range(2 * first_sub, 2 * nsub):
            s = _dot(qx[ch], kblk)
            if diag and ch // 2 == first_sub:
                s = jnp.where(col <= row, s, MASKED)
            m = m_ref[ch]
            m_new = jnp.maximum(m, jnp.max(s, axis=-1, keepdims=True))
            m_ref[ch] = m_new
            p = jnp.exp2(s - jnp.concatenate([m_new] * (blk // d2), axis=1)).astype(BF16)
            acc_ref[ch] = jnp.exp2(m - m_new) * acc_ref[ch] + _dot_nt(p, vx_ref[ch % 2, :, pl.ds(ks, blk)])
        return carry

    lax.fori_loop(0, nsub * qi, functools.partial(step, first_sub=0, diag=False), 0)
    for sb in range(nsub):
        step(nsub * qi + sb, 0, sb, True)
    for sb in range(nsub):
        acc0, acc1 = acc_ref[2 * sb], acc_ref[2 * sb + 1]
        o_ref[blk * sb:blk * (sb + 1), :] = jnp.where(
            lane < hd, acc0 / pltpu.roll(acc0, hd, 1), acc1 / pltpu.roll(acc1, hd, 1)).astype(o_ref.dtype)


def _attn_prompt(q, kTb, vTb, dc, blk, nsub):
    nb, d, length = kTb.shape
    n_heads = dc.shape[1]
    hd = d // n_heads
    nq = length // (blk * nsub)
    return pl.pallas_call(
        functools.partial(_attn_kernel, blk=blk, nsub=nsub, hd=hd),
        grid=(nb, n_heads // 2, nq),
        in_specs=[
            pl.BlockSpec((blk * nsub, 2 * hd), lambda b, hp, qi: (b * nq + qi, hp)),
            pl.BlockSpec((None, 2 * hd, length), lambda b, hp, qi: (b, hp, 0)),
            pl.BlockSpec((None, 2 * hd, length), lambda b, hp, qi: (b, hp, 0)),
            pl.BlockSpec((None, n_heads, length), lambda b, hp, qi: (b, 0, 0)),
        ],
        out_specs=pl.BlockSpec((blk * nsub, 2 * hd), lambda b, hp, qi: (b * nq + qi, hp)),
        out_shape=jax.ShapeDtypeStruct(q.shape, BF16),
        scratch_shapes=[pltpu.VMEM((4 * hd, length), BF16), pltpu.VMEM((2, 2 * hd, length), BF16),
                        pltpu.VMEM((2 * nsub, blk, 2 * hd), F32), pltpu.VMEM((2 * nsub, blk, 2 * hd), F32)],
        compiler_params=_params("parallel", "parallel", "arbitrary"),
        name="fox_prompt",
    )(q, kTb, vTb, dc)


def _decode_kernel(pt_ref, q_ref, kn_ref, vn_ref, lfn_ref, *refs, pages_per_step, hd):
    del pt_ref
    g = pages_per_step
    k_refs, v_refs, lf_refs = refs[:g], refs[g:2 * g], refs[2 * g:3 * g]
    o_ref, q8_ref, m_ref, l_ref, acc_ref, suf_ref = refs[3 * g:]
    n_heads, d = q8_ref.shape
    s_idx = pl.program_id(1)
    head_of_lane = lax.broadcasted_iota(jnp.int32, (n_heads, d), 1) // hd
    head_of_row = lax.broadcasted_iota(jnp.int32, (n_heads, d), 0)
    own = head_of_lane == head_of_row

    @pl.when(s_idx == 0)
    def _():
        q8 = jnp.where(own, jnp.broadcast_to(q_ref[0], (n_heads, d)), 0.0)
        q8_ref[...] = q8.astype(BF16)
        m_ref[...] = jnp.sum(q8 * kn_ref[0], axis=-1, keepdims=True)
        l_ref[...] = jnp.ones(l_ref.shape, F32)
        acc_ref[...] = jnp.broadcast_to(vn_ref[0], (n_heads, d))
        suf_ref[...] = lfn_ref[0]

    lane = lax.broadcasted_iota(jnp.int32, (n_heads, LANES), 1)
    q8 = q8_ref[...]
    m, l, acc, suf = m_ref[...], l_ref[...], acc_ref[...], suf_ref[...]
    logits = []
    for i in range(g):
        lf = lf_refs[i][0, 0]
        x = lf
        sh = 1
        while sh < LANES:
            x = x + jnp.where(lane < LANES - sh, pltpu.roll(x, LANES - sh, 1), 0.0)
            sh *= 2
        bias = (x - lf) + suf
        suf = suf + x[:, 0:1]
        kt = k_refs[i][0, 0].reshape(d, LANES).astype(BF16)
        logits.append(_dot(q8, kt) + LOG2E * bias)
    s = jnp.concatenate(logits, axis=1)
    m_new = jnp.maximum(m, jnp.max(s, axis=-1, keepdims=True))
    alpha = jnp.exp2(m - m_new)
    p = jnp.exp2(s - m_new)
    l = alpha * l + jnp.sum(p, axis=-1, keepdims=True)
    p = p.astype(BF16)
    pv = None
    for i in range(g):
        vt = v_refs[i][0, 0].reshape(d, LANES).astype(BF16)
        term = _dot_nt(p[:, LANES * i:LANES * (i + 1)], vt)
        pv = term if pv is None else pv + term
    acc = alpha * acc + pv
    m_ref[...], l_ref[...], acc_ref[...], suf_ref[...] = m_new, l, acc, suf

    @pl.when(s_idx == pl.num_programs(1) - 1)
    def _():
        o_ref[0] = jnp.sum(jnp.where(own, acc / l, 0.0), axis=0, keepdims=True).astype(o_ref.dtype)


def _attn_decode(q, k_new, v_new, lf_new, ck_t, cv_t, clf_t, page_table, layer, pages_per_step):
    db, d = q.shape
    n_heads, hd, page = ck_t.shape[2:]
    n_pages = page_table.shape[1]
    g = pages_per_step
    steps = n_pages // g

    def page_of(b, s, pt, i):
        return pt[b, n_pages - 1 - (s * g + i)]

    kv_specs = [pl.BlockSpec((1, 1, n_heads, hd, page),
                             lambda b, s, pt, i=i: (layer, page_of(b, s, pt, i), 0, 0, 0)) for i in range(g)]
    lf_specs = [pl.BlockSpec((1, 1, n_heads, page),
                             lambda b, s, pt, i=i: (layer, page_of(b, s, pt, i), 0, 0)) for i in range(g)]
    row_spec = pl.BlockSpec((1, 1, d), lambda b, s, pt: (b, 0, 0))
    grid_spec = pltpu.PrefetchScalarGridSpec(
        num_scalar_prefetch=1,
        grid=(db, steps),
        in_specs=[row_spec, row_spec, row_spec, pl.BlockSpec((1, n_heads, 1), lambda b, s, pt: (b, 0, 0))]
                 + kv_specs + kv_specs + lf_specs,
        out_specs=row_spec,
        scratch_shapes=[
            pltpu.VMEM((n_heads, d), BF16),
            pltpu.VMEM((n_heads, 1), F32),
            pltpu.VMEM((n_heads, 1), F32),
            pltpu.VMEM((n_heads, d), F32),
            pltpu.VMEM((n_heads, 1), F32),
        ],
    )
    out = pl.pallas_call(
        functools.partial(_decode_kernel, pages_per_step=g, hd=hd),
        grid_spec=grid_spec,
        out_shape=jax.ShapeDtypeStruct((db, 1, d), BF16),
        compiler_params=_params("parallel", "arbitrary"),
        name="fox_decode",
    )(page_table, q.reshape(db, 1, d), k_new.reshape(db, 1, d), v_new.reshape(db, 1, d),
      lf_new.reshape(db, n_heads, 1), *([ck_t] * g), *([cv_t] * g), *([clf_t] * g))
    return out.reshape(db, d)


def _pw1_kernel(x_ref, g_ref, w_ref, b_ref, o_ref):
    h = _rms_rows(x_ref[...], g_ref[...]).astype(BF16)
    z = _dot(h, w_ref[...]) + b_ref[...]
    d = o_ref.shape[1]
    o_ref[...] = z[:, :d] * _sigmoid(z[:, d:])


def _pw1(x2d, tm, g, w, b):
    rows, dm = x2d.shape
    dc = w.shape[1] // 2
    return pl.pallas_call(
        _pw1_kernel,
        grid=(rows // tm,),
        in_specs=[pl.BlockSpec((tm, dm), lambda i: (i, 0)), _resident(g.shape), _resident(w.shape),
                  _resident(b.shape)],
        out_specs=pl.BlockSpec((tm, dc), lambda i: (i, 0)),
        out_shape=jax.ShapeDtypeStruct((rows, dc), F32),
        compiler_params=_params("parallel"),
        name="conv_pw1_glu",
    )(x2d, g, w, b)


def _ln_swish(y, g, b):
    mu = jnp.mean(y, axis=-1, keepdims=True)
    yc = y - mu
    y = yc * lax.rsqrt(jnp.mean(yc * yc, axis=-1, keepdims=True) + EPS) * g + b
    return y * _sigmoid(y)


def _conv_kernel(g_ref, w_ref, bdw_ref, lng_ref, lnb_ref, o_ref, buf, ybuf, *, tl, rg, width):
    t = pl.program_id(0)
    bsz, _, dch = g_ref.shape
    hist = width - 1
    nch = dch // LANES
    jg = min(4, rg)

    @pl.when(t == 0)
    def _():
        buf[:, 0:hist * bsz, :] = jnp.zeros((nch, hist * bsz, LANES), F32)

    @pl.when(t > 0)
    def _():
        buf[:, 0:hist * bsz, :] = buf[:, tl * bsz:(tl + hist) * bsz, :]

    for b in range(bsz):
        for c in range(nch):
            buf[c, pl.ds(hist * bsz + b, tl, stride=bsz), :] = g_ref[b, :, LANES * c:LANES * (c + 1)]

    def chunk(ci, carry):
        r0 = pl.multiple_of(ci * (rg * bsz), rg * bsz)

        def lane_chunk(c, carry2):
            bias = jnp.broadcast_to(bdw_ref[c], (bsz, LANES))
            for j0 in range(0, rg, jg):
                accs = [bias] * jg
                for k in range(width):
                    tap = jnp.broadcast_to(w_ref[c, k:k + 1, :], (bsz, LANES))
                    for j in range(jg):
                        accs[j] = accs[j] + tap * buf[c, pl.ds(r0 + (j0 + j + k) * bsz, bsz), :]
                for j in range(jg):
                    ybuf[c, pl.ds(r0 + (j0 + j) * bsz, bsz), :] = accs[j]
            return carry2

        lax.fori_loop(0, nch, lane_chunk, 0)
        rows = pl.ds(r0, rg * bsz)
        y = ybuf[:, rows, :]
        mu = jnp.sum(jnp.sum(y, axis=0), axis=-1, keepdims=True) / dch
        yc = y - mu[None]
        var = jnp.sum(jnp.sum(yc * yc, axis=0), axis=-1, keepdims=True) / dch
        y = yc * lax.rsqrt(var + EPS)[None] * lng_ref[...] + lnb_ref[...]
        ybuf[:, rows, :] = y * _sigmoid(y)
        return carry

    lax.fori_loop(0, tl // rg, chunk, 0)
    for b in range(bsz):
        for c in range(nch):
            o_ref[b, :, LANES * c:LANES * (c + 1)] = ybuf[c, pl.ds(b, tl, stride=bsz), :].astype(o_ref.dtype)


def _conv_prompt(g3d, tl, w_pad, bdw, lng, lnb, width):
    nb, length, dch = g3d.shape
    hist = width - 1
    nch = dch // LANES
    lng, lnb, bdw = (v.reshape(nch, 1, LANES) for v in (lng, lnb, bdw))
    w_pad = jnp.transpose(w_pad.reshape(-1, nch, LANES), (1, 0, 2))
    return pl.pallas_call(
        functools.partial(_conv_kernel, tl=tl, rg=min(8, tl), width=width),
        grid=(length // tl,),
        in_specs=[pl.BlockSpec((nb, tl, dch), lambda t: (0, t, 0)), _resident(w_pad.shape),
                  _resident(bdw.shape), _resident(lng.shape), _resident(lnb.shape)],
        out_specs=pl.BlockSpec((nb, tl, dch), lambda t: (0, t, 0)),
        out_shape=jax.ShapeDtypeStruct((nb, length, dch), BF16),
        scratch_shapes=[pltpu.VMEM((nch, (hist + tl) * nb, LANES), F32), pltpu.VMEM((nch, tl * nb, LANES), F32)],
        compiler_params=_params("arbitrary"),
        name="conv_dw_prompt",
    )(g3d, w_pad, bdw, lng, lnb)


def _conv_step_kernel(hist_ref, g_ref, w_ref, bdw_ref, lng_ref, lnb_ref, o_ref, *, width):
    acc = bdw_ref[...] + w_ref[width - 1:width, :] * g_ref[...]
    for k in range(width - 1):
        acc = acc + w_ref[k:k + 1, :] * hist_ref[k]
    o_ref[...] = _ln_swish(acc, lng_ref[...], lnb_ref[...]).astype(o_ref.dtype)


def _conv_sample(hist_t, g2d, w_pad, bdw, lng, lnb, width):
    return pl.pallas_call(
        functools.partial(_conv_step_kernel, width=width),
        out_shape=jax.ShapeDtypeStruct(g2d.shape, BF16),
        compiler_params=pltpu.CompilerParams(vmem_limit_bytes=VMEM_LIMIT),
        name="conv_dw_sample",
    )(hist_t, g2d, w_pad, bdw, lng, lnb)


def _mixffn_kernel(*refs, n_mix, has_bias, ff_chunks):
    x_ref = refs[0]
    acts = refs[1:1 + n_mix]
    wo_ref = refs[1 + n_mix]
    pos = 2 + n_mix
    bias_ref = refs[pos] if has_bias else None
    pos += int(has_bias)
    gn_ref, wg_ref, wu_ref, wd_ref, o_ref = refs[pos:pos + 5]
    mixed = acts[0][...] if n_mix == 1 else jnp.concatenate([a[...] for a in acts], axis=1)
    x1 = x_ref[...] + _dot(mixed, wo_ref[...])
    if has_bias:
        x1 = x1 + bias_ref[...]
    h = _rms_rows(x1, gn_ref[...]).astype(BF16)
    acc = x1
    for c0, c1 in ff_chunks:
        gate = _dot(h, wg_ref[:, c0:c1])
        up = _dot(h, wu_ref[:, c0:c1])
        acc = acc + _dot((gate * _sigmoid(gate) * up).astype(BF16), wd_ref[c0:c1, :])
    o_ref[...] = acc


def _ff_chunks(dff, width=1024):
    return tuple((c, min(c + width, dff)) for c in range(0, dff, width))


def _mix_ffn(x2d, nb, tm, mix_acts, wo, bias, gn, wg, wu, wd):
    rows, dm = x2d.shape
    lr = rows // nb
    nt = lr // tm
    in_specs = [pl.BlockSpec((tm, dm), lambda b, t: (b * nt + t, 0))]
    for a in mix_acts:
        if a.shape[0] == rows:
            in_specs.append(pl.BlockSpec((tm, a.shape[1]), lambda b, t: (b * nt + t, 0)))
        else:
            in_specs.append(pl.BlockSpec((tm, a.shape[1] // nb), lambda b, t: (t, b)))
    consts = [wo] + ([bias] if bias is not None else []) + [gn, wg, wu, wd]
    in_specs += [_resident(c.shape) for c in consts]
    return pl.pallas_call(
        functools.partial(_mixffn_kernel, n_mix=len(mix_acts), has_bias=bias is not None,
                          ff_chunks=_ff_chunks(wg.shape[1])),
        grid=(nb, nt),
        in_specs=in_specs,
        out_specs=pl.BlockSpec((tm, dm), lambda b, t: (b * nt + t, 0)),
        out_shape=jax.ShapeDtypeStruct((rows, dm), F32),
        compiler_params=_params("parallel", "parallel"),
        name="mix_ffn",
    )(x2d, *mix_acts, *consts)


def kernel(x_prompt, x_sample, cache_k, cache_v, cache_logf, page_table, state_ssm_re, state_ssm_im, state_conv, norm_mix, norm_ffn, w_in, b_f, q_norm, k_norm, lam_re, lam_im, log_step, b_re, b_im, c_re, c_im, d_skip, w_glu, b_glu, w_out, w_pw1, b_pw1, w_dw, b_dw, ln_g, ln_b, w_pw2, b_pw2, w_gate, w_up, w_down):
    nb, length, dm = x_prompt.shape
    db = x_sample.shape[0]
    depth = norm_mix.shape[0]
    n_heads, hd = cache_k.shape[3], cache_k.shape[4]
    d_b = n_heads * hd
    d_a = w_in.shape[2] - 3 * d_b - n_heads
    n_state = state_ssm_re.shape[2] * state_ssm_re.shape[3]
    width = w_dw.shape[1]
    tm_p = min(512, length)
    blk_attn = min(256, length)
    nsub_attn = 2 if length % (2 * blk_attn) == 0 else 1
    tl_s5 = min(64, length)
    tl_conv = min(64, length)
    pages_per_step = min(16, page_table.shape[1])

    ck_t = jnp.transpose(cache_k, (0, 1, 3, 4, 2))
    cv_t = jnp.transpose(cache_v, (0, 1, 3, 4, 2))
    clf_t = jnp.transpose(cache_logf, (0, 1, 3, 2))

    seg = jnp.kron(jnp.eye(n_heads, dtype=F32), jnp.full((hd, hd), 1.0 / hd, F32)).astype(BF16)
    row = lambda v: v.reshape(1, -1).astype(F32)

    yp = x_prompt.reshape(nb * length, dm)
    ys = x_sample.reshape(db, dm)
    n_even = (depth + 1) // 2
    kv_stacks = None
    outs = {k: [] for k in ("lfp", "srp", "sip", "cvp", "ks", "vs", "lfs", "srs", "sis", "cvs")}

    for layer in range(depth):
        gn = row(norm_ffn[layer])
        wg, wu, wd = (w_gate[layer].astype(BF16), w_up[layer].astype(BF16), w_down[layer].astype(BF16))
        if layer % 2 == 0:
            e = layer // 2
            w = w_in[e]
            wuq = w[:, :d_a + d_b].astype(BF16)
            wkvT = w[:, d_a + d_b:d_a + 3 * d_b].T.astype(BF16)
            wfT = jnp.zeros((16, dm), F32).at[:n_heads].set(w[:, d_a + 3 * d_b:].T).astype(BF16)
            proj = (row(norm_mix[layer]), wuq, wkvT, wfT, b_f[e].reshape(n_heads, 1),
                    row(jnp.tile(q_norm[e], n_heads) * (hd ** -0.5 * LOG2E)), k_norm[e].reshape(hd, 1), seg, n_heads)
            s5w = _s5_weights(lam_re[e], lam_im[e], log_step[e], b_re[e], b_im[e], c_re[e], c_im[e], d_skip[e])
            glu = (w_glu[e].astype(BF16), row(b_glu[e]))
            wo = w_out[e].astype(BF16)

            u, q, kT_all, vT_all, kTb, vTb, lfT = _in_proj(yp, nb, tm_p, *proj, slot=e, n_slots=n_even,
                                                           stacks=kv_stacks)
            kv_stacks = (kT_all, vT_all)
            zeros = jnp.zeros((nb, n_state), F32)
            s_out, hr, hi = _s5(u.reshape(nb, length, d_a), zeros, zeros, *s5w, *glu, bsz=nb, tl=tl_s5)
            dc = _cumsum_rows(lfT.reshape(nb * n_heads, length)).reshape(nb, n_heads, length)
            a_out = _attn_prompt(q, kTb, vTb, dc, blk_attn, nsub_attn)
            yp = _mix_ffn(yp, nb, tm_p, [s_out.reshape(nb * length, d_a), a_out], wo, None,
                          gn, wg, wu, wd)
            outs["lfp"].append(jnp.transpose(lfT, (0, 2, 1)))
            outs["srp"].append(hr.reshape(nb, -1, state_ssm_re.shape[3]))
            outs["sip"].append(hi.reshape(nb, -1, state_ssm_re.shape[3]))

            u, q, kT, vT, _, _, lfT = _in_proj(ys, 1, db, *proj)
            k_new, v_new, lf_new = kT[0, 0].T, vT[0, 0].T, lfT[0].T
            s_out, hr, hi = _s5(u, state_ssm_re[e].reshape(db, n_state), state_ssm_im[e].reshape(db, n_state),
                                *s5w, *glu, bsz=db, tl=1)
            a_out = _attn_decode(q.astype(F32), k_new, v_new, lf_new, ck_t, cv_t, clf_t, page_table, e,
                                 pages_per_step)
            ys = _mix_ffn(ys, 1, db, [s_out, a_out], wo, None, gn, wg, wu, wd)
            outs["ks"].append(k_new.reshape(db, 1, n_heads, hd))
            outs["vs"].append(v_new.reshape(db, 1, n_heads, hd))
            outs["lfs"].append(lf_new.reshape(db, 1, n_heads))
            outs["srs"].append(hr.reshape(db, -1, state_ssm_re.shape[3]))
            outs["sis"].append(hi.reshape(db, -1, state_ssm_re.shape[3]))
        else:
            o = layer // 2
            gm = row(norm_mix[layer])
            w1, b1 = w_pw1[o].astype(BF16), row(b_pw1[o])
            w_pad = jnp.zeros((-(-width // 8) * 8, dm), F32).at[:width].set(w_dw[o])
            dw = (w_pad, row(b_dw[o]), row(ln_g[o]), row(ln_b[o]))
            w2, b2 = w_pw2[o].astype(BF16), row(b_pw2[o])

            g = _pw1(yp, tm_p, gm, w1, b1)
            act = _conv_prompt(g.reshape(nb, length, dm), tl_conv, *dw, width=width).reshape(nb * length, dm)
            yp = _mix_ffn(yp, nb, tm_p, [act], w2, b2, gn, wg, wu, wd)
            outs["cvp"].append(g.reshape(nb, length, dm)[:, length - (width - 1):])

            g = _pw1(ys, db, gm, w1, b1)
            hist = state_conv[o]
            act = _conv_sample(jnp.transpose(hist, (1, 0, 2)), g, *dw, width=width)
            ys = _mix_ffn(ys, 1, db, [act], w2, b2, gn, wg, wu, wd)
            outs["cvs"].append(jnp.concatenate([hist[:, 1:], g[:, None, :]], axis=1))

    st = lambda k: jnp.stack(outs[k])
    kv_out = [jnp.transpose(a.reshape(n_even, nb, n_heads, hd, length), (0, 1, 4, 2, 3)) for a in kv_stacks]
    return (yp.reshape(nb, length, dm), ys.reshape(db, 1, dm),
            kv_out[0], kv_out[1], st("lfp"), st("srp"), st("sip"), st("cvp"),
            st("ks"), st("vs"), st("lfs"), st("srs"), st("sis"), st("cvs"))
```

```python
import functools
import math
from typing import NamedTuple

import numpy as np
import jax
import jax.numpy as jnp
from jax import lax
from jax.experimental import pallas as pl
from jax.experimental.pallas import tpu as pltpu

F32 = jnp.float32
BF16 = jnp.bfloat16
EPS = 1e-6
MASKED = -1e30
LANES = 128
VMEM_LIMIT = 56 * 1024 * 1024
SQRT_2_OVER_PI = float(np.float32(math.sqrt(2.0 / math.pi)))
LOG2E = math.log2(math.e)


def _dot(a, b):
    return jnp.dot(a, b, preferred_element_type=F32)


def _dot_nt(a, b):
    return lax.dot_general(a, b, (((1,), (1,)), ((), ())), preferred_element_type=F32)


def _sigmoid(x):
    return 1.0 / (1.0 + jnp.exp(-x))


def _rms_rows(x, g):
    return x * lax.rsqrt(jnp.mean(x * x, axis=-1, keepdims=True) + EPS) * g


def _params(*sem):
    return pltpu.CompilerParams(dimension_semantics=sem, vmem_limit_bytes=VMEM_LIMIT)


def _resident(shape):
    zeros = (0,) * len(shape)
    return pl.BlockSpec(shape, lambda *_: zeros, pipeline_mode=pl.Buffered(1))


class _LayerOf(NamedTuple):
    stack: jax.Array
    index: int

    @property
    def shape(self):
        return self.stack.shape[1:]


def _array(x):
    return x.stack if isinstance(x, _LayerOf) else x


def _resident_op(x):
    if not isinstance(x, _LayerOf):
        return _resident(x.shape)
    idx = (x.index,) + (0,) * len(x.shape)
    return pl.BlockSpec((None,) + x.shape, lambda *_: idx, pipeline_mode=pl.Buffered(1))


def _inproj_kernel(x_ref, g_ref, wuq_ref, wkvT_ref, wfT_ref, bf_ref, qn_ref, kn_ref, seg_ref, *refs, n_heads):
    u_ref, q_ref, kT_ref, vT_ref, kTb_ref, vTb_ref, lfT_ref = refs[-7:]
    h = _rms_rows(x_ref[...], g_ref[...]).astype(BF16)
    z = _dot(h, wuq_ref[...])
    d = q_ref.shape[1]
    u_ref[...] = z[:, :z.shape[1] - d]
    q = z[:, z.shape[1] - d:]
    q_ms = _dot((q * q).astype(BF16), seg_ref[...])
    q_ref[...] = (q * lax.rsqrt(q_ms + EPS) * qn_ref[...]).astype(BF16)
    zT = _dot_nt(wkvT_ref[...], h)
    tm = zT.shape[1]
    k3 = zT[:d].reshape(n_heads, d // n_heads, tm)
    k3 = k3 * lax.rsqrt(jnp.mean(k3 * k3, axis=1, keepdims=True) + EPS) * kn_ref[...][None]
    kT = k3.reshape(d, tm)
    kT_ref[...] = kT
    kTb_ref[...] = kT.astype(BF16)
    vT = zT[d:]
    vT_ref[...] = vT
    vTb_ref[...] = vT.astype(BF16)
    f = _dot_nt(wfT_ref[...], h)[:n_heads] + bf_ref[...]
    lfT_ref[...] = jnp.minimum(f, 0.0) - jnp.log1p(jnp.exp(-jnp.abs(f)))


def _in_proj(x2d, nb, tm, g, wuq, wkvT, wfT, bf_col, qn_row, kn_col, seg, n_heads, slot=0, n_slots=1, stacks=None):
    rows, dm = x2d.shape
    lr = rows // nb
    nt = lr // tm
    d = qn_row.shape[1]
    da = wuq.shape[1] - d
    out_shape = (
        jax.ShapeDtypeStruct((rows, da), F32),
        jax.ShapeDtypeStruct((rows, d), BF16),
        jax.ShapeDtypeStruct((n_slots, nb, d, lr), F32),
        jax.ShapeDtypeStruct((n_slots, nb, d, lr), F32),
        jax.ShapeDtypeStruct((nb, d, lr), BF16),
        jax.ShapeDtypeStruct((nb, d, lr), BF16),
        jax.ShapeDtypeStruct((nb, n_heads, lr), F32),
    )
    t_spec = pl.BlockSpec((None, d, tm), lambda b, t: (b, 0, t))
    slot_spec = pl.BlockSpec((None, None, d, tm), lambda b, t: (slot, b, 0, t))
    row_spec = lambda width: pl.BlockSpec((tm, width), lambda b, t: (b * nt + t, 0))
    consts = (g, wuq, wkvT, wfT, bf_col, qn_row, kn_col, seg)
    extra = () if stacks is None else tuple(stacks)
    return pl.pallas_call(
        functools.partial(_inproj_kernel, n_heads=n_heads),
        grid=(nb, nt),
        in_specs=[row_spec(dm)] + [_resident(c.shape) for c in consts]
                 + [pl.BlockSpec(memory_space=pl.ANY)] * len(extra),
        out_specs=(row_spec(da), row_spec(d), slot_spec, slot_spec, t_spec, t_spec,
                   pl.BlockSpec((None, n_heads, tm), lambda b, t: (b, 0, t))),
        out_shape=out_shape,
        input_output_aliases={1 + len(consts) + i: 2 + i for i in range(len(extra))},
        compiler_params=_params("parallel", "parallel"),
        name="in_proj",
    )(x2d, *consts, *extra)


def _s5_kernel(u_ref, h0r_ref, h0i_ref, wb_ref, lbr_ref, lbi_ref, wc_ref, dsk_ref, wglu_ref, bglu_ref,
               s_ref, hr_out, hi_out, hbuf, hst_r, hst_i, uil, sil, *, bsz, tl):
    nblk, cb, sc2 = wb_ref.shape
    sc = sc2 // 2
    interleave = len(u_ref.shape) == 3

    @pl.when(pl.program_id(0) == 0)
    def _():
        hst_r[...] = h0r_ref[...]
        hst_i[...] = h0i_ref[...]

    if interleave:
        for b in range(bsz):
            for j in range(nblk):
                uil[j, pl.ds(b, tl, stride=bsz), :] = u_ref[b, :, cb * j:cb * (j + 1)]
    else:
        for j in range(nblk):
            uil[j] = u_ref[:, cb * j:cb * (j + 1)]

    for j in range(nblk):
        hbuf[:, sc2 * j:sc2 * (j + 1)] = _dot(uil[j].astype(BF16), wb_ref[j])

    for j in range(nblk):
        lre = jnp.broadcast_to(lbr_ref[:, sc * j:sc * (j + 1)], (bsz, sc))
        lim = jnp.broadcast_to(lbi_ref[:, sc * j:sc * (j + 1)], (bsz, sc))
        c_re, c_im = sc2 * j, sc2 * j + sc

        def step(t, carry, lre=lre, lim=lim, c_re=c_re, c_im=c_im):
            hr, hi = carry
            r0 = pl.multiple_of(t * bsz, bsz)
            nr = lre * hr - lim * hi + hbuf[pl.ds(r0, bsz), c_re:c_re + sc]
            ni = lre * hi + lim * hr + hbuf[pl.ds(r0, bsz), c_im:c_im + sc]
            hbuf[pl.ds(r0, bsz), c_re:c_re + sc] = nr
            hbuf[pl.ds(r0, bsz), c_im:c_im + sc] = ni
            return nr, ni

        hr, hi = lax.fori_loop(0, tl, step, (hst_r[:, sc * j:sc * (j + 1)], hst_i[:, sc * j:sc * (j + 1)]),
                               unroll=min(tl, 4))
        hst_r[:, sc * j:sc * (j + 1)] = hr
        hst_i[:, sc * j:sc * (j + 1)] = hi

    ys = []
    for j in range(nblk):
        hj = hbuf[:, sc2 * j:sc2 * (j + 1)].astype(BF16)
        ys.append(_dot(hj, wc_ref[j]) + dsk_ref[:, cb * j:cb * (j + 1)] * uil[j])
    y = jnp.concatenate(ys, axis=1)
    y = 0.5 * y * (1.0 + jnp.tanh(SQRT_2_OVER_PI * (y + 0.044715 * (y * y * y))))
    z = _dot(y.astype(BF16), wglu_ref[...]) + bglu_ref[...]
    out = y * _sigmoid(z)
    if interleave:
        for j in range(nblk):
            sil[j] = out[:, cb * j:cb * (j + 1)]
        for b in range(bsz):
            for j in range(nblk):
                s_ref[b, :, cb * j:cb * (j + 1)] = sil[j, pl.ds(b, tl, stride=bsz), :].astype(s_ref.dtype)
    else:
        s_ref[...] = out.astype(s_ref.dtype)
    hr_out[...] = hst_r[...]
    hi_out[...] = hst_i[...]


def _s5(u, h0r, h0i, wb, lbr, lbi, wc, dsk, wglu, bglu, bsz, tl):
    nstate = h0r.shape[1]
    da = u.shape[-1]
    blk = tl * bsz
    if u.ndim == 3:
        steps = u.shape[1] // tl
        io_spec = pl.BlockSpec((bsz, tl, da), lambda t: (0, t, 0))
    else:
        steps = u.shape[0] // blk
        io_spec = pl.BlockSpec((blk, da), lambda t: (t, 0))
    nblk, cb = wb.shape[0], wb.shape[1]
    return pl.pallas_call(
        functools.partial(_s5_kernel, bsz=bsz, tl=tl),
        grid=(steps,),
        in_specs=[
            io_spec,
            _resident(h0r.shape), _resident(h0i.shape), _resident(wb.shape), _resident(lbr.shape),
            _resident(lbi.shape), _resident(wc.shape), _resident(dsk.shape), _resident_op(wglu),
            _resident(bglu.shape),
        ],
        out_specs=(
            io_spec,
            pl.BlockSpec(h0r.shape, lambda t: (0, 0)),
            pl.BlockSpec(h0i.shape, lambda t: (0, 0)),
        ),
        out_shape=(
            jax.ShapeDtypeStruct(u.shape, BF16),
            jax.ShapeDtypeStruct(h0r.shape, F32),
            jax.ShapeDtypeStruct(h0i.shape, F32),
        ),
        scratch_shapes=[
            pltpu.VMEM((blk, 2 * nstate), F32),
            pltpu.VMEM(h0r.shape, F32),
            pltpu.VMEM(h0i.shape, F32),
            pltpu.VMEM((nblk, blk, cb), F32),
            pltpu.VMEM((nblk, blk, cb), F32),
        ],
        compiler_params=_params("arbitrary"),
        name="s5_mixer",
    )(u, h0r, h0i, wb, lbr, lbi, wc, dsk, _array(wglu), bglu)


def _s5_weights(lam_re, lam_im, log_step, b_re, b_im, c_re, c_im, d_skip):
    n_groups, n_state = lam_re.shape
    p = b_re.shape[-1]
    gpb = LANES // p
    nblk = n_groups // gpb
    step = jnp.exp(log_step)[:, None]
    mag = jnp.exp(lam_re * step)
    lbr = mag * jnp.cos(lam_im * step)
    lbi = mag * jnp.sin(lam_im * step)
    den = lam_re * lam_re + lam_im * lam_im
    fr = ((lbr - 1.0) * lam_re + lbi * lam_im) / den
    fi = (lbi * lam_re - (lbr - 1.0) * lam_im) / den
    bbr = fr[..., None] * b_re - fi[..., None] * b_im
    bbi = fr[..., None] * b_im + fi[..., None] * b_re
    eye = jnp.eye(gpb, dtype=F32)

    def b_block(bb):
        t = bb.reshape(nblk, gpb, n_state, p)
        return jnp.einsum('jgnp,gh->jgphn', t, eye).reshape(nblk, gpb * p, gpb * n_state)

    def c_block(cc):
        t = cc.reshape(nblk, gpb, p, n_state)
        return jnp.einsum('jgpn,gh->jgnhp', t, eye).reshape(nblk, gpb * n_state, gpb * p)

    wb = jnp.concatenate([b_block(bbr), b_block(bbi)], axis=-1).astype(BF16)
    wc = jnp.concatenate([c_block(c_re), -c_block(c_im)], axis=1).astype(BF16)
    return (wb, lbr.reshape(1, n_groups * n_state), lbi.reshape(1, n_groups * n_state), wc,
            d_skip.reshape(1, n_groups * p))


def _cumsum_kernel(x_ref, o_ref):
    rows, length = x_ref.shape
    lane = lax.broadcasted_iota(jnp.int32, (rows, LANES), 1)
    carry = jnp.zeros((rows, 1), F32)
    for c in range(length // LANES):
        x = x_ref[:, LANES * c:LANES * (c + 1)]
        d = 1
        while d < LANES:
            x = x + jnp.where(lane >= d, pltpu.roll(x, d, 1), 0.0)
            d *= 2
        x = x + carry
        o_ref[:, LANES * c:LANES * (c + 1)] = x
        carry = x[:, LANES - 1:LANES]


def _cumsum_rows(x):
    return pl.pallas_call(
        _cumsum_kernel,
        out_shape=jax.ShapeDtypeStruct(x.shape, F32),
        name="logf_cumsum",
    )(x)


def _split3_bf16(x):
    hi = x.astype(BF16)
    r = x - hi.astype(F32)
    mid = r.astype(BF16)
    return hi, mid, (r - mid.astype(F32)).astype(BF16)


def _attn_kernel(q_ref, kT_ref, vT_ref, dc_ref, o_ref, kx_ref, vx_ref, m_ref, acc_ref, *, blk, nsub, hd):
    hp = pl.program_id(1)
    qi = pl.program_id(2)
    d2 = 2 * hd
    n_bias = 16
    length = kT_ref.shape[1]

    @pl.when(qi == 0)
    def _():
        kx_ref[0:d2, :] = kT_ref[...]
        pieces = []
        for hh in range(2):
            pieces += list(_split3_bf16(-LOG2E * dc_ref[pl.ds(2 * hp + hh, 1), :]))
        pieces.append(jnp.zeros((n_bias - len(pieces), length), BF16))
        kx_ref[d2:d2 + n_bias, :] = jnp.concatenate(pieces, axis=0)
        kx_ref[d2 + n_bias:, :] = jnp.zeros((kx_ref.shape[0] - d2 - n_bias, length), BF16)
        sub = lax.broadcasted_iota(jnp.int32, (d2, 1), 0)
        vt = vT_ref[...]
        for hh in range(2):
            own = (sub >= hd) if hh else (sub < hd)
            vx_ref[hh] = jnp.where(own, vt, jnp.ones_like(vt))

    lane = lax.broadcasted_iota(jnp.int32, (1, d2), 1)
    row = lax.broadcasted_iota(jnp.int32, (blk, blk), 0)
    col = lax.broadcasted_iota(jnp.int32, (blk, blk), 1)
    qx = []
    for sb in range(nsub):
        q = q_ref[blk * sb:blk * (sb + 1), :]
        for hh in range(2):
            in_head = (lane >= hd) if hh else (lane < hd)
            sel = (lane >= 3 * hh) & (lane < 3 * hh + 3)
            qx.append(jnp.concatenate(
                [jnp.where(in_head, q, jnp.zeros_like(q)),
                 jnp.broadcast_to(jnp.where(sel, 1.0, 0.0).astype(BF16), (blk, d2))], axis=1))

    m_ref[...] = jnp.full(m_ref.shape, MASKED, F32)
    acc_ref[...] = jnp.zeros(acc_ref.shape, F32)

    def step(kb, carry, first_sub, diag):
        ks = pl.multiple_of(kb * blk, blk)
        kblk = kx_ref[:, pl.ds(ks, blk)]
        for ch in range(2 * first_sub, 2 * nsub):
            s = _dot(qx[ch], kblk)
            if diag and ch // 2 == first_sub:
                s = jnp.where(col <= row, s, MASKED)
            m = m_ref[ch]
            m_new = jnp.maximum(m, jnp.max(s, axis=-1, keepdims=True))
            m_ref[ch] = m_new
            p = jnp.exp2(s - jnp.concatenate([m_new] * (blk // d2), axis=1)).astype(BF16)
            acc_ref[ch] = jnp.exp2(m - m_new) * acc_ref[ch] + _dot_nt(p, vx_ref[ch % 2, :, pl.ds(ks, blk)])
        return carry

    lax.fori_loop(0, nsub * qi, functools.partial(step, first_sub=0, diag=False), 0)
    for sb in range(nsub):
        step(nsub * qi + sb, 0, sb, True)
    for sb in range(nsub):
        acc0, acc1 = acc_ref[2 * sb], acc_ref[2 * sb + 1]
        o_ref[blk * sb:blk * (sb + 1), :] = jnp.where(
            lane < hd, acc0 / pltpu.roll(acc0, hd, 1), acc1 / pltpu.roll(acc1, hd, 1)).astype(o_ref.dtype)


def _attn_prompt(q, kTb, vTb, dc, blk, nsub):
    nb, d, length = kTb.shape
    n_heads = dc.shape[1]
    hd = d // n_heads
    nq = length // (blk * nsub)
    return pl.pallas_call(
        functools.partial(_attn_kernel, blk=blk, nsub=nsub, hd=hd),
        grid=(nb, n_heads // 2, nq),
        in_specs=[
            pl.BlockSpec((blk * nsub, 2 * hd), lambda b, hp, qi: (b * nq + qi, hp)),
            pl.BlockSpec((None, 2 * hd, length), lambda b, hp, qi: (b, hp, 0)),
            pl.BlockSpec((None, 2 * hd, length), lambda b, hp, qi: (b, hp, 0)),
            pl.BlockSpec((None, n_heads, length), lambda b, hp, qi: (b, 0, 0)),
        ],
        out_specs=pl.BlockSpec((blk * nsub, 2 * hd), lambda b, hp, qi: (b * nq + qi, hp)),
        out_shape=jax.ShapeDtypeStruct(q.shape, BF16),
        scratch_shapes=[pltpu.VMEM((4 * hd, length), BF16), pltpu.VMEM((2, 2 * hd, length), BF16),
                        pltpu.VMEM((2 * nsub, blk, 2 * hd), F32), pltpu.VMEM((2 * nsub, blk, 2 * hd), F32)],
        compiler_params=_params("parallel", "parallel", "arbitrary"),
        name="fox_prompt",
    )(q, kTb, vTb, dc)


def _decode_kernel(pt_ref, q_ref, kn_ref, vn_ref, lfn_ref, *refs, pages_per_step, hd):
    del pt_ref
    g = pages_per_step
    k_refs, v_refs, lf_refs = refs[:g], refs[g:2 * g], refs[2 * g:3 * g]
    o_ref, q8_ref, m_ref, l_ref, acc_ref, suf_ref = refs[3 * g:]
    n_heads, d = q8_ref.shape
    s_idx = pl.program_id(1)
    head_of_lane = lax.broadcasted_iota(jnp.int32, (n_heads, d), 1) // hd
    head_of_row = lax.broadcasted_iota(jnp.int32, (n_heads, d), 0)
    own = head_of_lane == head_of_row

    @pl.when(s_idx == 0)
    def _():
        q8 = jnp.where(own, jnp.broadcast_to(q_ref[0], (n_heads, d)), 0.0)
        q8_ref[...] = q8.astype(BF16)
        m_ref[...] = jnp.sum(q8 * kn_ref[0], axis=-1, keepdims=True)
        l_ref[...] = jnp.ones(l_ref.shape, F32)
        acc_ref[...] = jnp.broadcast_to(vn_ref[0], (n_heads, d))
        suf_ref[...] = lfn_ref[0]

    lane = lax.broadcasted_iota(jnp.int32, (n_heads, LANES), 1)
    q8 = q8_ref[...]
    m, l, acc, suf = m_ref[...], l_ref[...], acc_ref[...], suf_ref[...]
    logits = []
    for i in range(g):
        lf = lf_refs[i][0, 0]
        x = lf
        sh = 1
        while sh < LANES:
            x = x + jnp.where(lane < LANES - sh, pltpu.roll(x, LANES - sh, 1), 0.0)
            sh *= 2
        bias = (x - lf) + suf
        suf = suf + x[:, 0:1]
        kt = k_refs[i][0, 0].reshape(d, LANES).astype(BF16)
        logits.append(_dot(q8, kt) + LOG2E * bias)
    s = jnp.concatenate(logits, axis=1)
    m_new = jnp.maximum(m, jnp.max(s, axis=-1, keepdims=True))
    alpha = jnp.exp2(m - m_new)
    p = jnp.exp2(s - m_new)
    l = alpha * l + jnp.sum(p, axis=-1, keepdims=True)
    p = p.astype(BF16)
    pv = None
    for i in range(g):
        vt = v_refs[i][0, 0].reshape(d, LANES).astype(BF16)
        term = _dot_nt(p[:, LANES * i:LANES * (i + 1)], vt)
        pv = term if pv is None else pv + term
    acc = alpha * acc + pv
    m_ref[...], l_ref[...], acc_ref[...], suf_ref[...] = m_new, l, acc, suf

    @pl.when(s_idx == pl.num_programs(1) - 1)
    def _():
        o_ref[0] = jnp.sum(jnp.where(own, acc / l, 0.0), axis=0, keepdims=True).astype(o_ref.dtype)


def _attn_decode(q, k_new, v_new, lf_new, ck_t, cv_t, clf_t, page_table, layer, pages_per_step):
    db, d = q.shape
    n_heads, hd, page = ck_t.shape[2:]
    n_pages = page_table.shape[1]
    g = pages_per_step
    steps = n_pages // g

    def page_of(b, s, pt, i):
        return pt[b, n_pages - 1 - (s * g + i)]

    kv_specs = [pl.BlockSpec((1, 1, n_heads, hd, page),
                             lambda b, s, pt, i=i: (layer, page_of(b, s, pt, i), 0, 0, 0)) for i in range(g)]
    lf_specs = [pl.BlockSpec((1, 1, n_heads, page),
                             lambda b, s, pt, i=i: (layer, page_of(b, s, pt, i), 0, 0)) for i in range(g)]
    row_spec = pl.BlockSpec((1, 1, d), lambda b, s, pt: (b, 0, 0))
    grid_spec = pltpu.PrefetchScalarGridSpec(
        num_scalar_prefetch=1,
        grid=(db, steps),
        in_specs=[row_spec, row_spec, row_spec, pl.BlockSpec((1, n_heads, 1), lambda b, s, pt: (b, 0, 0))]
                 + kv_specs + kv_specs + lf_specs,
        out_specs=row_spec,
        scratch_shapes=[
            pltpu.VMEM((n_heads, d), BF16),
            pltpu.VMEM((n_heads, 1), F32),
            pltpu.VMEM((n_heads, 1), F32),
            pltpu.VMEM((n_heads, d), F32),
            pltpu.VMEM((n_heads, 1), F32),
        ],
    )
    out = pl.pallas_call(
        functools.partial(_decode_kernel, pages_per_step=g, hd=hd),
        grid_spec=grid_spec,
        out_shape=jax.ShapeDtypeStruct((db, 1, d), BF16),
        compiler_params=_params("parallel", "arbitrary"),
        name="fox_decode",
    )(page_table, q.reshape(db, 1, d), k_new.reshape(db, 1, d), v_new.reshape(db, 1, d),
      lf_new.reshape(db, n_heads, 1), *([ck_t] * g), *([cv_t] * g), *([clf_t] * g))
    return out.reshape(db, d)


def _pw1_kernel(x_ref, g_ref, w_ref, b_ref, o_ref):
    h = _rms_rows(x_ref[...], g_ref[...]).astype(BF16)
    z = _dot(h, w_ref[...]) + b_ref[...]
    d = o_ref.shape[1]
    o_ref[...] = z[:, :d] * _sigmoid(z[:, d:])


def _pw1(x2d, tm, g, w, b):
    rows, dm = x2d.shape
    dc = w.shape[1] // 2
    return pl.pallas_call(
        _pw1_kernel,
        grid=(rows // tm,),
        in_specs=[pl.BlockSpec((tm, dm), lambda i: (i, 0)), _resident(g.shape), _resident_op(w),
                  _resident(b.shape)],
        out_specs=pl.BlockSpec((tm, dc), lambda i: (i, 0)),
        out_shape=jax.ShapeDtypeStruct((rows, dc), F32),
        compiler_params=_params("parallel"),
        name="conv_pw1_glu",
    )(x2d, g, _array(w), b)


def _ln_swish(y, g, b):
    mu = jnp.mean(y, axis=-1, keepdims=True)
    yc = y - mu
    y = yc * lax.rsqrt(jnp.mean(yc * yc, axis=-1, keepdims=True) + EPS) * g + b
    return y * _sigmoid(y)


def _conv_kernel(g_ref, w_ref, bdw_ref, lng_ref, lnb_ref, o_ref, buf, ybuf, *, tl, rg, width):
    t = pl.program_id(0)
    bsz, _, dch = g_ref.shape
    hist = width - 1
    nch = dch // LANES
    jg = min(4, rg)

    @pl.when(t == 0)
    def _():
        buf[:, 0:hist * bsz, :] = jnp.zeros((nch, hist * bsz, LANES), F32)

    @pl.when(t > 0)
    def _():
        buf[:, 0:hist * bsz, :] = buf[:, tl * bsz:(tl + hist) * bsz, :]

    for b in range(bsz):
        for c in range(nch):
            buf[c, pl.ds(hist * bsz + b, tl, stride=bsz), :] = g_ref[b, :, LANES * c:LANES * (c + 1)]

    def chunk(ci, carry):
        r0 = pl.multiple_of(ci * (rg * bsz), rg * bsz)

        def lane_chunk(c, carry2):
            bias = jnp.broadcast_to(bdw_ref[c], (bsz, LANES))
            for j0 in range(0, rg, jg):
                accs = [bias] * jg
                for k in range(width):
                    tap = jnp.broadcast_to(w_ref[c, k:k + 1, :], (bsz, LANES))
                    for j in range(jg):
                        accs[j] = accs[j] + tap * buf[c, pl.ds(r0 + (j0 + j + k) * bsz, bsz), :]
                for j in range(jg):
                    ybuf[c, pl.ds(r0 + (j0 + j) * bsz, bsz), :] = accs[j]
            return carry2

        lax.fori_loop(0, nch, lane_chunk, 0)
        rows = pl.ds(r0, rg * bsz)
        y = ybuf[:, rows, :]
        mu = jnp.sum(jnp.sum(y, axis=0), axis=-1, keepdims=True) / dch
        yc = y - mu[None]
        var = jnp.sum(jnp.sum(yc * yc, axis=0), axis=-1, keepdims=True) / dch
        y = yc * lax.rsqrt(var + EPS)[None] * lng_ref[...] + lnb_ref[...]
        ybuf[:, rows, :] = y * _sigmoid(y)
        return carry

    lax.fori_loop(0, tl // rg, chunk, 0)
    for b in range(bsz):
        for c in range(nch):
            o_ref[b, :, LANES * c:LANES * (c + 1)] = ybuf[c, pl.ds(b, tl, stride=bsz), :].astype(o_ref.dtype)


def _conv_prompt(g3d, tl, w_pad, bdw, lng, lnb, width):
    nb, length, dch = g3d.shape
    hist = width - 1
    nch = dch // LANES
    lng, lnb, bdw = (v.reshape(nch, 1, LANES) for v in (lng, lnb, bdw))
    w_pad = jnp.transpose(w_pad.reshape(-1, nch, LANES), (1, 0, 2))
    return pl.pallas_call(
        functools.partial(_conv_kernel, tl=tl, rg=min(8, tl), width=width),
        grid=(length // tl,),
        in_specs=[pl.BlockSpec((nb, tl, dch), lambda t: (0, t, 0)), _resident(w_pad.shape),
                  _resident(bdw.shape), _resident(lng.shape), _resident(lnb.shape)],
        out_specs=pl.BlockSpec((nb, tl, dch), lambda t: (0, t, 0)),
        out_shape=jax.ShapeDtypeStruct((nb, length, dch), BF16),
        scratch_shapes=[pltpu.VMEM((nch, (hist + tl) * nb, LANES), F32), pltpu.VMEM((nch, tl * nb, LANES), F32)],
        compiler_params=_params("arbitrary"),
        name="conv_dw_prompt",
    )(g3d, w_pad, bdw, lng, lnb)


def _conv_step_kernel(hist_ref, g_ref, w_ref, bdw_ref, lng_ref, lnb_ref, o_ref, *, width):
    acc = bdw_ref[...] + w_ref[width - 1:width, :] * g_ref[...]
    for k in range(width - 1):
        acc = acc + w_ref[k:k + 1, :] * hist_ref[k]
    o_ref[...] = _ln_swish(acc, lng_ref[...], lnb_ref[...]).astype(o_ref.dtype)


def _conv_sample(hist_t, g2d, w_pad, bdw, lng, lnb, width):
    return pl.pallas_call(
        functools.partial(_conv_step_kernel, width=width),
        out_shape=jax.ShapeDtypeStruct(g2d.shape, BF16),
        compiler_params=pltpu.CompilerParams(vmem_limit_bytes=VMEM_LIMIT),
        name="conv_dw_sample",
    )(hist_t, g2d, w_pad, bdw, lng, lnb)


def _mixffn_kernel(*refs, n_mix, has_bias, ff_chunks):
    x_ref = refs[0]
    acts = refs[1:1 + n_mix]
    wo_ref = refs[1 + n_mix]
    pos = 2 + n_mix
    bias_ref = refs[pos] if has_bias else None
    pos += int(has_bias)
    gn_ref, wg_ref, wu_ref, wd_ref, o_ref = refs[pos:pos + 5]
    mixed = acts[0][...] if n_mix == 1 else jnp.concatenate([a[...] for a in acts], axis=1)
    x1 = x_ref[...] + _dot(mixed, wo_ref[...])
    if has_bias:
        x1 = x1 + bias_ref[...]
    h = _rms_rows(x1, gn_ref[...]).astype(BF16)
    acc = x1
    for c0, c1 in ff_chunks:
        gate = _dot(h, wg_ref[:, c0:c1])
        up = _dot(h, wu_ref[:, c0:c1])
        acc = acc + _dot((gate * _sigmoid(gate) * up).astype(BF16), wd_ref[c0:c1, :])
    o_ref[...] = acc


def _ff_chunks(dff, width=1024):
    return tuple((c, min(c + width, dff)) for c in range(0, dff, width))


def _mix_ffn(x2d, nb, tm, mix_acts, wo, bias, gn, wg, wu, wd):
    rows, dm = x2d.shape
    lr = rows // nb
    nt = lr // tm
    in_specs = [pl.BlockSpec((tm, dm), lambda b, t: (b * nt + t, 0))]
    for a in mix_acts:
        if a.shape[0] == rows:
            in_specs.append(pl.BlockSpec((tm, a.shape[1]), lambda b, t: (b * nt + t, 0)))
        else:
            in_specs.append(pl.BlockSpec((tm, a.shape[1] // nb), lambda b, t: (t, b)))
    consts = [wo] + ([bias] if bias is not None else []) + [gn, wg, wu, wd]
    in_specs += [_resident_op(c) for c in consts]
    return pl.pallas_call(
        functools.partial(_mixffn_kernel, n_mix=len(mix_acts), has_bias=bias is not None,
                          ff_chunks=_ff_chunks(wg.shape[1])),
        grid=(nb, nt),
        in_specs=in_specs,
        out_specs=pl.BlockSpec((tm, dm), lambda b, t: (b * nt + t, 0)),
        out_shape=jax.ShapeDtypeStruct((rows, dm), F32),
        compiler_params=_params("parallel", "parallel"),
        name="mix_ffn",
    )(x2d, *mix_acts, *[_array(c) for c in consts])


def kernel(x_prompt, x_sample, cache_k, cache_v, cache_logf, page_table, state_ssm_re, state_ssm_im, state_conv, norm_mix, norm_ffn, w_in, b_f, q_norm, k_norm, lam_re, lam_im, log_step, b_re, b_im, c_re, c_im, d_skip, w_glu, b_glu, w_out, w_pw1, b_pw1, w_dw, b_dw, ln_g, ln_b, w_pw2, b_pw2, w_gate, w_up, w_down):
    nb, length, dm = x_prompt.shape
    db = x_sample.shape[0]
    depth = norm_mix.shape[0]
    n_heads, hd = cache_k.shape[3], cache_k.shape[4]
    d_b = n_heads * hd
    d_a = w_in.shape[2] - 3 * d_b - n_heads
    n_state = state_ssm_re.shape[2] * state_ssm_re.shape[3]
    width = w_dw.shape[1]
    tm_p = min(512, length)
    blk_attn = min(512, length)
    nsub_attn = 2 if length % (2 * blk_attn) == 0 else 1
    tl_s5 = min(64, length)
    tl_conv = min(64, length)
    pages_per_step = min(16, page_table.shape[1])

    ck_t = jnp.transpose(cache_k, (0, 1, 3, 4, 2))
    cv_t = jnp.transpose(cache_v, (0, 1, 3, 4, 2))
    clf_t = jnp.transpose(cache_logf, (0, 1, 3, 2))

    seg = jnp.kron(jnp.eye(n_heads, dtype=F32), jnp.full((hd, hd), 1.0 / hd, F32)).astype(BF16)
    row = lambda v: v.reshape(1, -1).astype(F32)

    yp = x_prompt.reshape(nb * length, dm)
    ys = x_sample.reshape(db, dm)
    n_even = (depth + 1) // 2
    kv_stacks = None
    outs = {k: [] for k in ("lfp", "srp", "sip", "cvp", "ks", "vs", "lfs", "srs", "sis", "cvs")}

    wg_all, wu_all, wd_all, wglu_all, wout_all, wpw1_all, wpw2_all = (
        w.astype(BF16) for w in (w_gate, w_up, w_down, w_glu, w_out, w_pw1, w_pw2))

    for layer in range(depth):
        gn = row(norm_ffn[layer])
        wg, wu, wd = _LayerOf(wg_all, layer), _LayerOf(wu_all, layer), _LayerOf(wd_all, layer)
        if layer % 2 == 0:
            e = layer // 2
            w = w_in[e]
            wuq = w[:, :d_a + d_b].astype(BF16)
            wkvT = w[:, d_a + d_b:d_a + 3 * d_b].T.astype(BF16)
            wfT = jnp.zeros((16, dm), F32).at[:n_heads].set(w[:, d_a + 3 * d_b:].T).astype(BF16)
            proj = (row(norm_mix[layer]), wuq, wkvT, wfT, b_f[e].reshape(n_heads, 1),
                    row(jnp.tile(q_norm[e], n_heads) * (hd ** -0.5 * LOG2E)), k_norm[e].reshape(hd, 1), seg, n_heads)
            s5w = _s5_weights(lam_re[e], lam_im[e], log_step[e], b_re[e], b_im[e], c_re[e], c_im[e], d_skip[e])
            glu = (_LayerOf(wglu_all, e), row(b_glu[e]))
            wo = _LayerOf(wout_all, e)

            u, q, kT_all, vT_all, kTb, vTb, lfT = _in_proj(yp, nb, tm_p, *proj, slot=e, n_slots=n_even,
                                                           stacks=kv_stacks)
            kv_stacks = (kT_all, vT_all)
            zeros = jnp.zeros((nb, n_state), F32)
            s_out, hr, hi = _s5(u.reshape(nb, length, d_a), zeros, zeros, *s5w, *glu, bsz=nb, tl=tl_s5)
            dc = _cumsum_rows(lfT.reshape(nb * n_heads, length)).reshape(nb, n_heads, length)
            a_out = _attn_prompt(q, kTb, vTb, dc, blk_attn, nsub_attn)
            yp = _mix_ffn(yp, nb, tm_p, [s_out.reshape(nb * length, d_a), a_out], wo, None,
                          gn, wg, wu, wd)
            outs["lfp"].append(jnp.transpose(lfT, (0, 2, 1)))
            outs["srp"].append(hr.reshape(nb, -1, state_ssm_re.shape[3]))
            outs["sip"].append(hi.reshape(nb, -1, state_ssm_re.shape[3]))

            u, q, kT, vT, _, _, lfT = _in_proj(ys, 1, db, *proj)
            k_new, v_new, lf_new = kT[0, 0].T, vT[0, 0].T, lfT[0].T
            s_out, hr, hi = _s5(u, state_ssm_re[e].reshape(db, n_state), state_ssm_im[e].reshape(db, n_state),
                                *s5w, *glu, bsz=db, tl=1)
            a_out = _attn_decode(q.astype(F32), k_new, v_new, lf_new, ck_t, cv_t, clf_t, page_table, e,
                                 pages_per_step)
            ys = _mix_ffn(ys, 1, db, [s_out, a_out], wo, None, gn, wg, wu, wd)
            outs["ks"].append(k_new.reshape(db, 1, n_heads, hd))
            outs["vs"].append(v_new.reshape(db, 1, n_heads, hd))
            outs["lfs"].append(lf_new.reshape(db, 1, n_heads))
            outs["srs"].append(hr.reshape(db, -1, state_ssm_re.shape[3]))
            outs["sis"].append(hi.reshape(db, -1, state_ssm_re.shape[3]))
        else:
            o = layer // 2
            gm = row(norm_mix[layer])
            w1, b1 = _LayerOf(wpw1_all, o), row(b_pw1[o])
            w_pad = jnp.zeros((-(-width // 8) * 8, dm), F32).at[:width].set(w_dw[o])
            dw = (w_pad, row(b_dw[o]), row(ln_g[o]), row(ln_b[o]))
            w2, b2 = _LayerOf(wpw2_all, o), row(b_pw2[o])

            g = _pw1(yp, tm_p, gm, w1, b1)
            act = _conv_prompt(g.reshape(nb, length, dm), tl_conv, *dw, width=width).reshape(nb * length, dm)
            yp = _mix_ffn(yp, nb, tm_p, [act], w2, b2, gn, wg, wu, wd)
            outs["cvp"].append(g.reshape(nb, length, dm)[:, length - (width - 1):])

            g = _pw1(ys, db, gm, w1, b1)
            hist = state_conv[o]
            act = _conv_sample(jnp.transpose(hist, (1, 0, 2)), g, *dw, width=width)
            ys = _mix_ffn(ys, 1, db, [act], w2, b2, gn, wg, wu, wd)
            outs["cvs"].append(jnp.concatenate([hist[:, 1:], g[:, None, :]], axis=1))

    st = lambda k: jnp.stack(outs[k])
    kv_out = [jnp.transpose(a.reshape(n_even, nb, n_heads, hd, length), (0, 1, 4, 2, 3)) for a in kv_stacks]
    return (yp.reshape(nb, length, dm), ys.reshape(db, 1, dm),
            kv_out[0], kv_out[1], st("lfp"), st("srp"), st("sip"), st("cvp"),
            st("ks"), st("vs"), st("lfs"), st("srs"), st("sis"), st("cvs"))
```

```python
import functools
import math
from typing import NamedTuple

import numpy as np
import jax
import jax.numpy as jnp
from jax import lax
from jax.experimental import pallas as pl
from jax.experimental.pallas import tpu as pltpu

F32 = jnp.float32
BF16 = jnp.bfloat16
EPS = 1e-6
MASKED = -1e30
LANES = 128
VMEM_LIMIT = 56 * 1024 * 1024
SQRT_2_OVER_PI = float(np.float32(math.sqrt(2.0 / math.pi)))
LOG2E = math.log2(math.e)


def _dot(a, b):
    return jnp.dot(a, b, preferred_element_type=F32)


def _dot_nt(a, b):
    return lax.dot_general(a, b, (((1,), (1,)), ((), ())), preferred_element_type=F32)


def _sigmoid(x):
    return 1.0 / (1.0 + jnp.exp(-x))


def _rms_rows(x, g):
    return x * lax.rsqrt(jnp.mean(x * x, axis=-1, keepdims=True) + EPS) * g


def _params(*sem):
    return pltpu.CompilerParams(dimension_semantics=sem, vmem_limit_bytes=VMEM_LIMIT)


def _resident(shape):
    zeros = (0,) * len(shape)
    return pl.BlockSpec(shape, lambda *_: zeros, pipeline_mode=pl.Buffered(1))


class _LayerOf(NamedTuple):
    stack: jax.Array
    index: int

    @property
    def shape(self):
        return self.stack.shape[1:]


def _array(x):
    return x.stack if isinstance(x, _LayerOf) else x


def _resident_op(x):
    if not isinstance(x, _LayerOf):
        return _resident(x.shape)
    idx = (x.index,) + (0,) * len(x.shape)
    return pl.BlockSpec((None,) + x.shape, lambda *_: idx, pipeline_mode=pl.Buffered(1))


def _inproj_kernel(x_ref, g_ref, wuq_ref, wkvT_ref, wfT_ref, bf_ref, qn_ref, kn_ref, seg_ref, *refs, n_heads):
    u_ref, q_ref, kT_ref, vT_ref, kTb_ref, vTb_ref, lfT_ref = refs[-7:]
    h = _rms_rows(x_ref[...], g_ref[...]).astype(BF16)
    z = _dot(h, wuq_ref[...])
    d = q_ref.shape[1]
    u_ref[...] = z[:, :z.shape[1] - d]
    q = z[:, z.shape[1] - d:]
    q_ms = _dot((q * q).astype(BF16), seg_ref[...])
    q_ref[...] = (q * lax.rsqrt(q_ms + EPS) * qn_ref[...]).astype(BF16)
    zT = _dot_nt(wkvT_ref[...], h)
    tm = zT.shape[1]
    k3 = zT[:d].reshape(n_heads, d // n_heads, tm)
    k3 = k3 * lax.rsqrt(jnp.mean(k3 * k3, axis=1, keepdims=True) + EPS) * kn_ref[...][None]
    kT = k3.reshape(d, tm)
    kT_ref[...] = kT
    kTb_ref[...] = kT.astype(BF16)
    vT = zT[d:]
    vT_ref[...] = vT
    vTb_ref[...] = vT.astype(BF16)
    f = _dot_nt(wfT_ref[...], h)[:n_heads] + bf_ref[...]
    lfT_ref[...] = jnp.minimum(f, 0.0) - jnp.log1p(jnp.exp(-jnp.abs(f)))


def _in_proj(x2d, nb, tm, g, wuq, wkvT, wfT, bf_col, qn_row, kn_col, seg, n_heads, slot=0, n_slots=1, stacks=None):
    rows, dm = x2d.shape
    lr = rows // nb
    nt = lr // tm
    d = qn_row.shape[1]
    da = wuq.shape[1] - d
    out_shape = (
        jax.ShapeDtypeStruct((rows, da), F32),
        jax.ShapeDtypeStruct((rows, d), BF16),
        jax.ShapeDtypeStruct((n_slots, nb, d, lr), F32),
        jax.ShapeDtypeStruct((n_slots, nb, d, lr), F32),
        jax.ShapeDtypeStruct((nb, d, lr), BF16),
        jax.ShapeDtypeStruct((nb, d, lr), BF16),
        jax.ShapeDtypeStruct((nb, n_heads, lr), F32),
    )
    t_spec = pl.BlockSpec((None, d, tm), lambda b, t: (b, 0, t))
    slot_spec = pl.BlockSpec((None, None, d, tm), lambda b, t: (slot, b, 0, t))
    row_spec = lambda width: pl.BlockSpec((tm, width), lambda b, t: (b * nt + t, 0))
    consts = (g, wuq, wkvT, wfT, bf_col, qn_row, kn_col, seg)
    extra = () if stacks is None else tuple(stacks)
    return pl.pallas_call(
        functools.partial(_inproj_kernel, n_heads=n_heads),
        grid=(nb, nt),
        in_specs=[row_spec(dm)] + [_resident(c.shape) for c in consts]
                 + [pl.BlockSpec(memory_space=pl.ANY)] * len(extra),
        out_specs=(row_spec(da), row_spec(d), slot_spec, slot_spec, t_spec, t_spec,
                   pl.BlockSpec((None, n_heads, tm), lambda b, t: (b, 0, t))),
        out_shape=out_shape,
        input_output_aliases={1 + len(consts) + i: 2 + i for i in range(len(extra))},
        compiler_params=_params("parallel", "parallel"),
        name="in_proj",
    )(x2d, *consts, *extra)


def _s5_kernel(u_ref, h0r_ref, h0i_ref, wb_ref, lbr_ref, lbi_ref, wc_ref, dsk_ref, wglu_ref, bglu_ref,
               s_ref, hr_out, hi_out, hbuf, hst_r, hst_i, uil, sil, *, bsz, tl):
    nblk, cb, sc2 = wb_ref.shape
    sc = sc2 // 2
    interleave = len(u_ref.shape) == 3

    @pl.when(pl.program_id(0) == 0)
    def _():
        hst_r[...] = h0r_ref[...]
        hst_i[...] = h0i_ref[...]

    if interleave:
        for b in range(bsz):
            for j in range(nblk):
                uil[j, pl.ds(b, tl, stride=bsz), :] = u_ref[b, :, cb * j:cb * (j + 1)]
    else:
        for j in range(nblk):
            uil[j] = u_ref[:, cb * j:cb * (j + 1)]

    for j in range(nblk):
        hbuf[:, sc2 * j:sc2 * (j + 1)] = _dot(uil[j].astype(BF16), wb_ref[j])

    for j in range(nblk):
        lre = jnp.broadcast_to(lbr_ref[:, sc * j:sc * (j + 1)], (bsz, sc))
        lim = jnp.broadcast_to(lbi_ref[:, sc * j:sc * (j + 1)], (bsz, sc))
        c_re, c_im = sc2 * j, sc2 * j + sc

        def step(t, carry, lre=lre, lim=lim, c_re=c_re, c_im=c_im):
            hr, hi = carry
            r0 = pl.multiple_of(t * bsz, bsz)
            nr = lre * hr - lim * hi + hbuf[pl.ds(r0, bsz), c_re:c_re + sc]
            ni = lre * hi + lim * hr + hbuf[pl.ds(r0, bsz), c_im:c_im + sc]
            hbuf[pl.ds(r0, bsz), c_re:c_re + sc] = nr
            hbuf[pl.ds(r0, bsz), c_im:c_im + sc] = ni
            return nr, ni

        hr, hi = lax.fori_loop(0, tl, step, (hst_r[:, sc * j:sc * (j + 1)], hst_i[:, sc * j:sc * (j + 1)]),
                               unroll=min(tl, 4))
        hst_r[:, sc * j:sc * (j + 1)] = hr
        hst_i[:, sc * j:sc * (j + 1)] = hi

    ys = []
    for j in range(nblk):
        hj = hbuf[:, sc2 * j:sc2 * (j + 1)].astype(BF16)
        ys.append(_dot(hj, wc_ref[j]) + dsk_ref[:, cb * j:cb * (j + 1)] * uil[j])
    y = jnp.concatenate(ys, axis=1)
    y = 0.5 * y * (1.0 + jnp.tanh(SQRT_2_OVER_PI * (y + 0.044715 * (y * y * y))))
    z = _dot(y.astype(BF16), wglu_ref[...]) + bglu_ref[...]
    out = y * _sigmoid(z)
    if interleave:
        for j in range(nblk):
            sil[j] = out[:, cb * j:cb * (j + 1)]
        for b in range(bsz):
            for j in range(nblk):
                s_ref[b, :, cb * j:cb * (j + 1)] = sil[j, pl.ds(b, tl, stride=bsz), :].astype(s_ref.dtype)
    else:
        s_ref[...] = out.astype(s_ref.dtype)
    hr_out[...] = hst_r[...]
    hi_out[...] = hst_i[...]


def _s5(u, h0r, h0i, wb, lbr, lbi, wc, dsk, wglu, bglu, bsz, tl):
    nstate = h0r.shape[1]
    da = u.shape[-1]
    blk = tl * bsz
    if u.ndim == 3:
        steps = u.shape[1] // tl
        io_spec = pl.BlockSpec((bsz, tl, da), lambda t: (0, t, 0))
    else:
        steps = u.shape[0] // blk
        io_spec = pl.BlockSpec((blk, da), lambda t: (t, 0))
    nblk, cb = wb.shape[0], wb.shape[1]
    return pl.pallas_call(
        functools.partial(_s5_kernel, bsz=bsz, tl=tl),
        grid=(steps,),
        in_specs=[
            io_spec,
            _resident(h0r.shape), _resident(h0i.shape), _resident(wb.shape), _resident(lbr.shape),
            _resident(lbi.shape), _resident(wc.shape), _resident(dsk.shape), _resident_op(wglu),
            _resident(bglu.shape),
        ],
        out_specs=(
            io_spec,
            pl.BlockSpec(h0r.shape, lambda t: (0, 0)),
            pl.BlockSpec(h0i.shape, lambda t: (0, 0)),
        ),
        out_shape=(
            jax.ShapeDtypeStruct(u.shape, BF16),
            jax.ShapeDtypeStruct(h0r.shape, F32),
            jax.ShapeDtypeStruct(h0i.shape, F32),
        ),
        scratch_shapes=[
            pltpu.VMEM((blk, 2 * nstate), F32),
            pltpu.VMEM(h0r.shape, F32),
            pltpu.VMEM(h0i.shape, F32),
            pltpu.VMEM((nblk, blk, cb), F32),
            pltpu.VMEM((nblk, blk, cb), F32),
        ],
        compiler_params=_params("arbitrary"),
        name="s5_mixer",
    )(u, h0r, h0i, wb, lbr, lbi, wc, dsk, _array(wglu), bglu)


def _s5_weights(lam_re, lam_im, log_step, b_re, b_im, c_re, c_im, d_skip):
    n_groups, n_state = lam_re.shape
    p = b_re.shape[-1]
    gpb = LANES // p
    nblk = n_groups // gpb
    step = jnp.exp(log_step)[:, None]
    mag = jnp.exp(lam_re * step)
    lbr = mag * jnp.cos(lam_im * step)
    lbi = mag * jnp.sin(lam_im * step)
    den = lam_re * lam_re + lam_im * lam_im
    fr = ((lbr - 1.0) * lam_re + lbi * lam_im) / den
    fi = (lbi * lam_re - (lbr - 1.0) * lam_im) / den
    bbr = fr[..., None] * b_re - fi[..., None] * b_im
    bbi = fr[..., None] * b_im + fi[..., None] * b_re
    eye = jnp.eye(gpb, dtype=F32)

    def b_block(bb):
        t = bb.reshape(nblk, gpb, n_state, p)
        return jnp.einsum('jgnp,gh->jgphn', t, eye).reshape(nblk, gpb * p, gpb * n_state)

    def c_block(cc):
        t = cc.reshape(nblk, gpb, p, n_state)
        return jnp.einsum('jgpn,gh->jgnhp', t, eye).reshape(nblk, gpb * n_state, gpb * p)

    wb = jnp.concatenate([b_block(bbr), b_block(bbi)], axis=-1).astype(BF16)
    wc = jnp.concatenate([c_block(c_re), -c_block(c_im)], axis=1).astype(BF16)
    return (wb, lbr.reshape(1, n_groups * n_state), lbi.reshape(1, n_groups * n_state), wc,
            d_skip.reshape(1, n_groups * p))


def _cumsum_kernel(x_ref, o_ref):
    rows, length = x_ref.shape
    lane = lax.broadcasted_iota(jnp.int32, (rows, LANES), 1)
    carry = jnp.zeros((rows, 1), F32)
    for c in range(length // LANES):
        x = x_ref[:, LANES * c:LANES * (c + 1)]
        d = 1
        while d < LANES:
            x = x + jnp.where(lane >= d, pltpu.roll(x, d, 1), 0.0)
            d *= 2
        x = x + carry
        o_ref[:, LANES * c:LANES * (c + 1)] = x
        carry = x[:, LANES - 1:LANES]


def _cumsum_rows(x):
    return pl.pallas_call(
        _cumsum_kernel,
        out_shape=jax.ShapeDtypeStruct(x.shape, F32),
        name="logf_cumsum",
    )(x)


def _split3_bf16(x):
    hi = x.astype(BF16)
    r = x - hi.astype(F32)
    mid = r.astype(BF16)
    return hi, mid, (r - mid.astype(F32)).astype(BF16)


def _attn_kernel(q_ref, kT_ref, vT_ref, dc_ref, o_ref, kx_ref, vx_ref, m_ref, acc_ref, *, blk, nsub, hd):
    hp = pl.program_id(1)
    qi = pl.program_id(2)
    d2 = 2 * hd
    n_bias = 16
    length = kT_ref.shape[1]

    @pl.when(qi == 0)
    def _():
        kx_ref[0:d2, :] = kT_ref[...]
        pieces = []
        for hh in range(2):
            pieces += list(_split3_bf16(-LOG2E * dc_ref[pl.ds(2 * hp + hh, 1), :]))
        pieces.append(jnp.zeros((n_bias - len(pieces), length), BF16))
        kx_ref[d2:d2 + n_bias, :] = jnp.concatenate(pieces, axis=0)
        kx_ref[d2 + n_bias:, :] = jnp.zeros((kx_ref.shape[0] - d2 - n_bias, length), BF16)
        sub = lax.broadcasted_iota(jnp.int32, (d2, 1), 0)
        vt = vT_ref[...]
        for hh in range(2):
            own = (sub >= hd) if hh else (sub < hd)
            vx_ref[hh] = jnp.where(own, vt, jnp.ones_like(vt))

    lane = lax.broadcasted_iota(jnp.int32, (1, d2), 1)
    row = lax.broadcasted_iota(jnp.int32, (blk, blk), 0)
    col = lax.broadcasted_iota(jnp.int32, (blk, blk), 1)
    qx = []
    for sb in range(nsub):
        q = q_ref[blk * sb:blk * (sb + 1), :]
        for hh in range(2):
            in_head = (lane >= hd) if hh else (lane < hd)
            sel = (lane >= 3 * hh) & (lane < 3 * hh + 3)
            qx.append(jnp.concatenate(
                [jnp.where(in_head, q, jnp.zeros_like(q)),
                 jnp.broadcast_to(jnp.where(sel, 1.0, 0.0).astype(BF16), (blk, d2))], axis=1))

    m_ref[...] = jnp.full(m_ref.shape, MASKED, F32)
    acc_ref[...] = jnp.zeros(acc_ref.shape, F32)

    def step(kb, carry, first_sub, diag):
        ks = pl.multiple_of(kb * blk, blk)
        kblk = kx_ref[:, pl.ds(ks, blk)]
        for ch in range(2 * first_sub, 2 * nsub):
            s = _dot(qx[ch], kblk)
            if diag and ch // 2 == first_sub:
                s = jnp.where(col <= row, s, MASKED)
            m = m_ref[ch]
            m_new = jnp.maximum(m, jnp.max(s, axis=-1, keepdims=True))
            m_ref[ch] = m_new
            p = jnp.exp2(s - jnp.concatenate([m_new] * (blk // d2), axis=1)).astype(BF16)
            acc_ref[ch] = jnp.exp2(m - m_new) * acc_ref[ch] + _dot_nt(p, vx_ref[ch % 2, :, pl.ds(ks, blk)])
        return carry

    lax.fori_loop(0, nsub * qi, functools.partial(step, first_sub=0, diag=False), 0)
    for sb in range(nsub):
        step(nsub * qi + sb, 0, sb, True)
    for sb in range(nsub):
        acc0, acc1 = acc_ref[2 * sb], acc_ref[2 * sb + 1]
        o_ref[blk * sb:blk * (sb + 1), :] = jnp.where(
            lane < hd, acc0 / pltpu.roll(acc0, hd, 1), acc1 / pltpu.roll(acc1, hd, 1)).astype(o_ref.dtype)


def _attn_prompt(q, kTb, vTb, dc, blk, nsub):
    nb, d, length = kTb.shape
    n_heads = dc.shape[1]
    hd = d // n_heads
    nq = length // (blk * nsub)
    return pl.pallas_call(
        functools.partial(_attn_kernel, blk=blk, nsub=nsub, hd=hd),
        grid=(nb, n_heads // 2, nq),
        in_specs=[
            pl.BlockSpec((blk * nsub, 2 * hd), lambda b, hp, qi: (b * nq + qi, hp)),
            pl.BlockSpec((None, 2 * hd, length), lambda b, hp, qi: (b, hp, 0)),
            pl.BlockSpec((None, 2 * hd, length), lambda b, hp, qi: (b, hp, 0)),
            pl.BlockSpec((None, n_heads, length), lambda b, hp, qi: (b, 0, 0)),
        ],
        out_specs=pl.BlockSpec((blk * nsub, 2 * hd), lambda b, hp, qi: (b * nq + qi, hp)),
        out_shape=jax.ShapeDtypeStruct(q.shape, BF16),
        scratch_shapes=[pltpu.VMEM((4 * hd, length), BF16), pltpu.VMEM((2, 2 * hd, length), BF16),
                        pltpu.VMEM((2 * nsub, blk, 2 * hd), F32), pltpu.VMEM((2 * nsub, blk, 2 * hd), F32)],
        compiler_params=_params("parallel", "parallel", "arbitrary"),
        name="fox_prompt",
    )(q, kTb, vTb, dc)


def _decode_kernel(pt_ref, q_ref, kn_ref, vn_ref, lfn_ref, *refs, pages_per_step, hd):
    del pt_ref
    g = pages_per_step
    k_refs, v_refs, lf_refs = refs[:g], refs[g:2 * g], refs[2 * g:3 * g]
    o_ref, q8_ref, m_ref, l_ref, acc_ref, suf_ref = refs[3 * g:]
    n_heads, d = q8_ref.shape
    s_idx = pl.program_id(1)
    head_of_lane = lax.broadcasted_iota(jnp.int32, (n_heads, d), 1) // hd
    head_of_row = lax.broadcasted_iota(jnp.int32, (n_heads, d), 0)
    own = head_of_lane == head_of_row

    @pl.when(s_idx == 0)
    def _():
        q8 = jnp.where(own, jnp.broadcast_to(q_ref[0], (n_heads, d)), 0.0)
        q8_ref[...] = q8.astype(BF16)
        m_ref[...] = jnp.sum(q8 * kn_ref[0], axis=-1, keepdims=True)
        l_ref[...] = jnp.ones(l_ref.shape, F32)
        acc_ref[...] = jnp.broadcast_to(vn_ref[0], (n_heads, d))
        suf_ref[...] = lfn_ref[0]

    lane = lax.broadcasted_iota(jnp.int32, (n_heads, LANES), 1)
    q8 = q8_ref[...]
    m, l, acc, suf = m_ref[...], l_ref[...], acc_ref[...], suf_ref[...]
    logits = []
    for i in range(g):
        lf = lf_refs[i][0, 0]
        x = lf
        sh = 1
        while sh < LANES:
            x = x + jnp.where(lane < LANES - sh, pltpu.roll(x, LANES - sh, 1), 0.0)
            sh *= 2
        bias = (x - lf) + suf
        suf = suf + x[:, 0:1]
        kt = k_refs[i][0, 0].reshape(d, LANES).astype(BF16)
        logits.append(_dot(q8, kt) + LOG2E * bias)
    s = jnp.concatenate(logits, axis=1)
    m_new = jnp.maximum(m, jnp.max(s, axis=-1, keepdims=True))
    alpha = jnp.exp2(m - m_new)
    p = jnp.exp2(s - m_new)
    l = alpha * l + jnp.sum(p, axis=-1, keepdims=True)
    p = p.astype(BF16)
    pv = None
    for i in range(g):
        vt = v_refs[i][0, 0].reshape(d, LANES).astype(BF16)
        term = _dot_nt(p[:, LANES * i:LANES * (i + 1)], vt)
        pv = term if pv is None else pv + term
    acc = alpha * acc + pv
    m_ref[...], l_ref[...], acc_ref[...], suf_ref[...] = m_new, l, acc, suf

    @pl.when(s_idx == pl.num_programs(1) - 1)
    def _():
        o_ref[0] = jnp.sum(jnp.where(own, acc / l, 0.0), axis=0, keepdims=True).astype(o_ref.dtype)


def _attn_decode(q, k_new, v_new, lf_new, ck_t, cv_t, clf_t, page_table, layer, pages_per_step):
    db, d = q.shape
    n_heads, hd, page = ck_t.shape[2:]
    n_pages = page_table.shape[1]
    g = pages_per_step
    steps = n_pages // g

    def page_of(b, s, pt, i):
        return pt[b, n_pages - 1 - (s * g + i)]

    kv_specs = [pl.BlockSpec((1, 1, n_heads, hd, page),
                             lambda b, s, pt, i=i: (layer, page_of(b, s, pt, i), 0, 0, 0)) for i in range(g)]
    lf_specs = [pl.BlockSpec((1, 1, n_heads, page),
                             lambda b, s, pt, i=i: (layer, page_of(b, s, pt, i), 0, 0)) for i in range(g)]
    row_spec = pl.BlockSpec((1, 1, d), lambda b, s, pt: (b, 0, 0))
    grid_spec = pltpu.PrefetchScalarGridSpec(
        num_scalar_prefetch=1,
        grid=(db, steps),
        in_specs=[row_spec, row_spec, row_spec, pl.BlockSpec((1, n_heads, 1), lambda b, s, pt: (b, 0, 0))]
                 + kv_specs + kv_specs + lf_specs,
        out_specs=row_spec,
        scratch_shapes=[
            pltpu.VMEM((n_heads, d), BF16),
            pltpu.VMEM((n_heads, 1), F32),
            pltpu.VMEM((n_heads, 1), F32),
            pltpu.VMEM((n_heads, d), F32),
            pltpu.VMEM((n_heads, 1), F32),
        ],
    )
    out = pl.pallas_call(
        functools.partial(_decode_kernel, pages_per_step=g, hd=hd),
        grid_spec=grid_spec,
        out_shape=jax.ShapeDtypeStruct((db, 1, d), BF16),
        compiler_params=_params("parallel", "arbitrary"),
        name="fox_decode",
    )(page_table, q.reshape(db, 1, d), k_new.reshape(db, 1, d), v_new.reshape(db, 1, d),
      lf_new.reshape(db, n_heads, 1), *([ck_t] * g), *([cv_t] * g), *([clf_t] * g))
    return out.reshape(db, d)


def _pw1_kernel(x_ref, g_ref, w_ref, b_ref, o_ref):
    h = _rms_rows(x_ref[...], g_ref[...]).astype(BF16)
    z = _dot(h, w_ref[...]) + b_ref[...]
    d = o_ref.shape[1]
    o_ref[...] = z[:, :d] * _sigmoid(z[:, d:])


def _pw1(x2d, tm, g, w, b):
    rows, dm = x2d.shape
    dc = w.shape[1] // 2
    return pl.pallas_call(
        _pw1_kernel,
        grid=(rows // tm,),
        in_specs=[pl.BlockSpec((tm, dm), lambda i: (i, 0)), _resident(g.shape), _resident_op(w),
                  _resident(b.shape)],
        out_specs=pl.BlockSpec((tm, dc), lambda i: (i, 0)),
        out_shape=jax.ShapeDtypeStruct((rows, dc), F32),
        compiler_params=_params("parallel"),
        name="conv_pw1_glu",
    )(x2d, g, _array(w), b)


def _ln_swish(y, g, b):
    mu = jnp.mean(y, axis=-1, keepdims=True)
    yc = y - mu
    y = yc * lax.rsqrt(jnp.mean(yc * yc, axis=-1, keepdims=True) + EPS) * g + b
    return y * _sigmoid(y)


def _conv_kernel(g_ref, w_ref, bdw_ref, lng_ref, lnb_ref, o_ref, buf, ybuf, *, tl, rg, width):
    t = pl.program_id(0)
    bsz, _, dch = g_ref.shape
    hist = width - 1
    nch = dch // LANES
    jg = min(8, rg)

    @pl.when(t == 0)
    def _():
        buf[:, 0:hist * bsz, :] = jnp.zeros((nch, hist * bsz, LANES), F32)

    @pl.when(t > 0)
    def _():
        buf[:, 0:hist * bsz, :] = buf[:, tl * bsz:(tl + hist) * bsz, :]

    for b in range(bsz):
        for c in range(nch):
            buf[c, pl.ds(hist * bsz + b, tl, stride=bsz), :] = g_ref[b, :, LANES * c:LANES * (c + 1)]

    def chunk(ci, carry):
        r0 = pl.multiple_of(ci * (rg * bsz), rg * bsz)

        def lane_chunk(c, carry2):
            bias = jnp.broadcast_to(bdw_ref[c], (bsz, LANES))
            for j0 in range(0, rg, jg):
                accs = [bias] * jg
                for k in range(width):
                    tap = jnp.broadcast_to(w_ref[c, k:k + 1, :], (bsz, LANES))
                    for j in range(jg):
                        accs[j] = accs[j] + tap * buf[c, pl.ds(r0 + (j0 + j + k) * bsz, bsz), :]
                for j in range(jg):
                    ybuf[c, pl.ds(r0 + (j0 + j) * bsz, bsz), :] = accs[j]
            return carry2

        lax.fori_loop(0, nch, lane_chunk, 0)
        rows = pl.ds(r0, rg * bsz)
        y = ybuf[:, rows, :]
        mu = jnp.sum(jnp.sum(y, axis=0), axis=-1, keepdims=True) / dch
        yc = y - mu[None]
        var = jnp.sum(jnp.sum(yc * yc, axis=0), axis=-1, keepdims=True) / dch
        y = yc * lax.rsqrt(var + EPS)[None] * lng_ref[...] + lnb_ref[...]
        ybuf[:, rows, :] = y * _sigmoid(y)
        return carry

    lax.fori_loop(0, tl // rg, chunk, 0)
    for b in range(bsz):
        for c in range(nch):
            o_ref[b, :, LANES * c:LANES * (c + 1)] = ybuf[c, pl.ds(b, tl, stride=bsz), :].astype(o_ref.dtype)


def _conv_prompt(g3d, tl, w_pad, bdw, lng, lnb, width):
    nb, length, dch = g3d.shape
    hist = width - 1
    nch = dch // LANES
    lng, lnb, bdw = (v.reshape(nch, 1, LANES) for v in (lng, lnb, bdw))
    w_pad = jnp.transpose(w_pad.reshape(-1, nch, LANES), (1, 0, 2))
    return pl.pallas_call(
        functools.partial(_conv_kernel, tl=tl, rg=min(8, tl), width=width),
        grid=(length // tl,),
        in_specs=[pl.BlockSpec((nb, tl, dch), lambda t: (0, t, 0)), _resident(w_pad.shape),
                  _resident(bdw.shape), _resident(lng.shape), _resident(lnb.shape)],
        out_specs=pl.BlockSpec((nb, tl, dch), lambda t: (0, t, 0)),
        out_shape=jax.ShapeDtypeStruct((nb, length, dch), BF16),
        scratch_shapes=[pltpu.VMEM((nch, (hist + tl) * nb, LANES), F32), pltpu.VMEM((nch, tl * nb, LANES), F32)],
        compiler_params=_params("arbitrary"),
        name="conv_dw_prompt",
    )(g3d, w_pad, bdw, lng, lnb)


def _conv_step_kernel(hist_ref, g_ref, w_ref, bdw_ref, lng_ref, lnb_ref, o_ref, *, width):
    acc = bdw_ref[...] + w_ref[width - 1:width, :] * g_ref[...]
    for k in range(width - 1):
        acc = acc + w_ref[k:k + 1, :] * hist_ref[k]
    o_ref[...] = _ln_swish(acc, lng_ref[...], lnb_ref[...]).astype(o_ref.dtype)


def _conv_sample(hist_t, g2d, w_pad, bdw, lng, lnb, width):
    return pl.pallas_call(
        functools.partial(_conv_step_kernel, width=width),
        out_shape=jax.ShapeDtypeStruct(g2d.shape, BF16),
        compiler_params=pltpu.CompilerParams(vmem_limit_bytes=VMEM_LIMIT),
        name="conv_dw_sample",
    )(hist_t, g2d, w_pad, bdw, lng, lnb)


def _mixffn_kernel(*refs, n_mix, has_bias, ff_chunks):
    x_ref = refs[0]
    acts = refs[1:1 + n_mix]
    wo_ref = refs[1 + n_mix]
    pos = 2 + n_mix
    bias_ref = refs[pos] if has_bias else None
    pos += int(has_bias)
    gn_ref, wg_ref, wu_ref, wd_ref, o_ref = refs[pos:pos + 5]
    mixed = acts[0][...] if n_mix == 1 else jnp.concatenate([a[...] for a in acts], axis=1)
    x1 = x_ref[...] + _dot(mixed, wo_ref[...])
    if has_bias:
        x1 = x1 + bias_ref[...]
    h = _rms_rows(x1, gn_ref[...]).astype(BF16)
    acc = x1
    for c0, c1 in ff_chunks:
        gate = _dot(h, wg_ref[:, c0:c1])
        up = _dot(h, wu_ref[:, c0:c1])
        acc = acc + _dot((gate * _sigmoid(gate) * up).astype(BF16), wd_ref[c0:c1, :])
    o_ref[...] = acc


def _ff_chunks(dff, width=1024):
    return tuple((c, min(c + width, dff)) for c in range(0, dff, width))


def _mix_ffn(x2d, nb, tm, mix_acts, wo, bias, gn, wg, wu, wd):
    rows, dm = x2d.shape
    lr = rows // nb
    nt = lr // tm
    in_specs = [pl.BlockSpec((tm, dm), lambda b, t: (b * nt + t, 0))]
    for a in mix_acts:
        if a.shape[0] == rows:
            in_specs.append(pl.BlockSpec((tm, a.shape[1]), lambda b, t: (b * nt + t, 0)))
        else:
            in_specs.append(pl.BlockSpec((tm, a.shape[1] // nb), lambda b, t: (t, b)))
    consts = [wo] + ([bias] if bias is not None else []) + [gn, wg, wu, wd]
    in_specs += [_resident_op(c) for c in consts]
    return pl.pallas_call(
        functools.partial(_mixffn_kernel, n_mix=len(mix_acts), has_bias=bias is not None,
                          ff_chunks=_ff_chunks(wg.shape[1])),
        grid=(nb, nt),
        in_specs=in_specs,
        out_specs=pl.BlockSpec((tm, dm), lambda b, t: (b * nt + t, 0)),
        out_shape=jax.ShapeDtypeStruct((rows, dm), F32),
        compiler_params=_params("parallel", "parallel"),
        name="mix_ffn",
    )(x2d, *mix_acts, *[_array(c) for c in consts])


def kernel(x_prompt, x_sample, cache_k, cache_v, cache_logf, page_table, state_ssm_re, state_ssm_im, state_conv, norm_mix, norm_ffn, w_in, b_f, q_norm, k_norm, lam_re, lam_im, log_step, b_re, b_im, c_re, c_im, d_skip, w_glu, b_glu, w_out, w_pw1, b_pw1, w_dw, b_dw, ln_g, ln_b, w_pw2, b_pw2, w_gate, w_up, w_down):
    nb, length, dm = x_prompt.shape
    db = x_sample.shape[0]
    depth = norm_mix.shape[0]
    n_heads, hd = cache_k.shape[3], cache_k.shape[4]
    d_b = n_heads * hd
    d_a = w_in.shape[2] - 3 * d_b - n_heads
    n_state = state_ssm_re.shape[2] * state_ssm_re.shape[3]
    width = w_dw.shape[1]
    tm_p = min(512, length)
    blk_attn = min(512, length)
    nsub_attn = 2 if length % (2 * blk_attn) == 0 else 1
    tl_s5 = min(64, length)
    tl_conv = min(64, length)
    pages_per_step = min(32, page_table.shape[1])

    ck_t = jnp.transpose(cache_k, (0, 1, 3, 4, 2))
    cv_t = jnp.transpose(cache_v, (0, 1, 3, 4, 2))
    clf_t = jnp.transpose(cache_logf, (0, 1, 3, 2))

    seg = jnp.kron(jnp.eye(n_heads, dtype=F32), jnp.full((hd, hd), 1.0 / hd, F32)).astype(BF16)
    row = lambda v: v.reshape(1, -1).astype(F32)

    yp = x_prompt.reshape(nb * length, dm)
    ys = x_sample.reshape(db, dm)
    n_even = (depth + 1) // 2
    kv_stacks = None
    outs = {k: [] for k in ("lfp", "srp", "sip", "cvp", "ks", "vs", "lfs", "srs", "sis", "cvs")}

    wg_all, wu_all, wd_all, wglu_all, wout_all, wpw1_all, wpw2_all = (
        w.astype(BF16) for w in (w_gate, w_up, w_down, w_glu, w_out, w_pw1, w_pw2))

    for layer in range(depth):
        gn = row(norm_ffn[layer])
        wg, wu, wd = _LayerOf(wg_all, layer), _LayerOf(wu_all, layer), _LayerOf(wd_all, layer)
        if layer % 2 == 0:
            e = layer // 2
            w = w_in[e]
            wuq = w[:, :d_a + d_b].astype(BF16)
            wkvT = w[:, d_a + d_b:d_a + 3 * d_b].T.astype(BF16)
            wfT = jnp.zeros((16, dm), F32).at[:n_heads].set(w[:, d_a + 3 * d_b:].T).astype(BF16)
            proj = (row(norm_mix[layer]), wuq, wkvT, wfT, b_f[e].reshape(n_heads, 1),
                    row(jnp.tile(q_norm[e], n_heads) * (hd ** -0.5 * LOG2E)), k_norm[e].reshape(hd, 1), seg, n_heads)
            s5w = _s5_weights(lam_re[e], lam_im[e], log_step[e], b_re[e], b_im[e], c_re[e], c_im[e], d_skip[e])
            glu = (_LayerOf(wglu_all, e), row(b_glu[e]))
            wo = _LayerOf(wout_all, e)

            u, q, kT_all, vT_all, kTb, vTb, lfT = _in_proj(yp, nb, tm_p, *proj, slot=e, n_slots=n_even,
                                                           stacks=kv_stacks)
            kv_stacks = (kT_all, vT_all)
            zeros = jnp.zeros((nb, n_state), F32)
            s_out, hr, hi = _s5(u.reshape(nb, length, d_a), zeros, zeros, *s5w, *glu, bsz=nb, tl=tl_s5)
            dc = _cumsum_rows(lfT.reshape(nb * n_heads, length)).reshape(nb, n_heads, length)
            a_out = _attn_prompt(q, kTb, vTb, dc, blk_attn, nsub_attn)
            yp = _mix_ffn(yp, nb, tm_p, [s_out.reshape(nb * length, d_a), a_out], wo, None,
                          gn, wg, wu, wd)
            outs["lfp"].append(jnp.transpose(lfT, (0, 2, 1)))
            outs["srp"].append(hr.reshape(nb, -1, state_ssm_re.shape[3]))
            outs["sip"].append(hi.reshape(nb, -1, state_ssm_re.shape[3]))

            u, q, kT, vT, _, _, lfT = _in_proj(ys, 1, db, *proj)
            k_new, v_new, lf_new = kT[0, 0].T, vT[0, 0].T, lfT[0].T
            s_out, hr, hi = _s5(u, state_ssm_re[e].reshape(db, n_state), state_ssm_im[e].reshape(db, n_state),
                                *s5w, *glu, bsz=db, tl=1)
            a_out = _attn_decode(q.astype(F32), k_new, v_new, lf_new, ck_t, cv_t, clf_t, page_table, e,
                                 pages_per_step)
            ys = _mix_ffn(ys, 1, db, [s_out, a_out], wo, None, gn, wg, wu, wd)
            outs["ks"].append(k_new.reshape(db, 1, n_heads, hd))
            outs["vs"].append(v_new.reshape(db, 1, n_heads, hd))
            outs["lfs"].append(lf_new.reshape(db, 1, n_heads))
            outs["srs"].append(hr.reshape(db, -1, state_ssm_re.shape[3]))
            outs["sis"].append(hi.reshape(db, -1, state_ssm_re.shape[3]))
        else:
            o = layer // 2
            gm = row(norm_mix[layer])
            w1, b1 = _LayerOf(wpw1_all, o), row(b_pw1[o])
            w_pad = jnp.zeros((-(-width // 8) * 8, dm), F32).at[:width].set(w_dw[o])
            dw = (w_pad, row(b_dw[o]), row(ln_g[o]), row(ln_b[o]))
            w2, b2 = _LayerOf(wpw2_all, o), row(b_pw2[o])

            g = _pw1(yp, tm_p, gm, w1, b1)
            act = _conv_prompt(g.reshape(nb, length, dm), tl_conv, *dw, width=width).reshape(nb * length, dm)
            yp = _mix_ffn(yp, nb, tm_p, [act], w2, b2, gn, wg, wu, wd)
            outs["cvp"].append(g.reshape(nb, length, dm)[:, length - (width - 1):])

            g = _pw1(ys, db, gm, w1, b1)
            hist = state_conv[o]
            act = _conv_sample(jnp.transpose(hist, (1, 0, 2)), g, *dw, width=width)
            ys = _mix_ffn(ys, 1, db, [act], w2, b2, gn, wg, wu, wd)
            outs["cvs"].append(jnp.concatenate([hist[:, 1:], g[:, None, :]], axis=1))

    st = lambda k: jnp.stack(outs[k])
    kv_out = [jnp.transpose(a.reshape(n_even, nb, n_heads, hd, length), (0, 1, 4, 2, 3)) for a in kv_stacks]
    return (yp.reshape(nb, length, dm), ys.reshape(db, 1, dm),
            kv_out[0], kv_out[1], st("lfp"), st("srp"), st("sip"), st("cvp"),
            st("ks"), st("vs"), st("lfs"), st("srs"), st("sis"), st("cvs"))
```

```python
import functools
import math
from typing import NamedTuple

import numpy as np
import jax
import jax.numpy as jnp
from jax import lax
from jax.experimental import pallas as pl
from jax.experimental.pallas import tpu as pltpu

F32 = jnp.float32
BF16 = jnp.bfloat16
EPS = 1e-6
MASKED = -1e30
LANES = 128
VMEM_LIMIT = 56 * 1024 * 1024
SQRT_2_OVER_PI = float(np.float32(math.sqrt(2.0 / math.pi)))
LOG2E = math.log2(math.e)


def _dot(a, b):
    return jnp.dot(a, b, preferred_element_type=F32)


def _dot_nt(a, b):
    return lax.dot_general(a, b, (((1,), (1,)), ((), ())), preferred_element_type=F32)


def _sigmoid(x):
    return 1.0 / (1.0 + jnp.exp(-x))


def _rms_rows(x, g):
    return x * lax.rsqrt(jnp.mean(x * x, axis=-1, keepdims=True) + EPS) * g


def _params(*sem):
    return pltpu.CompilerParams(dimension_semantics=sem, vmem_limit_bytes=VMEM_LIMIT)


def _resident(shape):
    zeros = (0,) * len(shape)
    return pl.BlockSpec(shape, lambda *_: zeros, pipeline_mode=pl.Buffered(1))


class _LayerOf(NamedTuple):
    stack: jax.Array
    index: int

    @property
    def shape(self):
        return self.stack.shape[1:]


def _array(x):
    return x.stack if isinstance(x, _LayerOf) else x


def _resident_op(x):
    if not isinstance(x, _LayerOf):
        return _resident(x.shape)
    idx = (x.index,) + (0,) * len(x.shape)
    return pl.BlockSpec((None,) + x.shape, lambda *_: idx, pipeline_mode=pl.Buffered(1))


def _inproj_kernel(x_ref, g_ref, wuq_ref, wkvT_ref, wfT_ref, bf_ref, qn_ref, kn_ref, seg_ref, *refs, n_heads):
    u_ref, q_ref, kT_ref, vT_ref, kTb_ref, vTb_ref, lfT_ref = refs[-7:]
    h = _rms_rows(x_ref[...], g_ref[...]).astype(BF16)
    z = _dot(h, wuq_ref[...])
    d = q_ref.shape[1]
    u_ref[...] = z[:, :z.shape[1] - d]
    q = z[:, z.shape[1] - d:]
    q_ms = _dot((q * q).astype(BF16), seg_ref[...])
    q_ref[...] = (q * lax.rsqrt(q_ms + EPS) * qn_ref[...]).astype(BF16)
    zT = _dot_nt(wkvT_ref[...], h)
    tm = zT.shape[1]
    k3 = zT[:d].reshape(n_heads, d // n_heads, tm)
    k3 = k3 * lax.rsqrt(jnp.mean(k3 * k3, axis=1, keepdims=True) + EPS) * kn_ref[...][None]
    kT = k3.reshape(d, tm)
    kT_ref[...] = kT
    kTb_ref[...] = kT.astype(BF16)
    vT = zT[d:]
    vT_ref[...] = vT
    vTb_ref[...] = vT.astype(BF16)
    f = _dot_nt(wfT_ref[...], h)[:n_heads] + bf_ref[...]
    lfT_ref[...] = jnp.minimum(f, 0.0) - jnp.log1p(jnp.exp(-jnp.abs(f)))


def _in_proj(x2d, nb, tm, g, wuq, wkvT, wfT, bf_col, qn_row, kn_col, seg, n_heads, slot=0, n_slots=1, stacks=None):
    rows, dm = x2d.shape
    lr = rows // nb
    nt = lr // tm
    d = qn_row.shape[1]
    da = wuq.shape[1] - d
    out_shape = (
        jax.ShapeDtypeStruct((rows, da), F32),
        jax.ShapeDtypeStruct((rows, d), BF16),
        jax.ShapeDtypeStruct((n_slots, nb, d, lr), F32),
        jax.ShapeDtypeStruct((n_slots, nb, d, lr), F32),
        jax.ShapeDtypeStruct((nb, d, lr), BF16),
        jax.ShapeDtypeStruct((nb, d, lr), BF16),
        jax.ShapeDtypeStruct((nb, n_heads, lr), F32),
    )
    t_spec = pl.BlockSpec((None, d, tm), lambda b, t: (b, 0, t))
    slot_spec = pl.BlockSpec((None, None, d, tm), lambda b, t: (slot, b, 0, t))
    row_spec = lambda width: pl.BlockSpec((tm, width), lambda b, t: (b * nt + t, 0))
    consts = (g, wuq, wkvT, wfT, bf_col, qn_row, kn_col, seg)
    extra = () if stacks is None else tuple(stacks)
    return pl.pallas_call(
        functools.partial(_inproj_kernel, n_heads=n_heads),
        grid=(nb, nt),
        in_specs=[row_spec(dm)] + [_resident(c.shape) for c in consts]
                 + [pl.BlockSpec(memory_space=pl.ANY)] * len(extra),
        out_specs=(row_spec(da), row_spec(d), slot_spec, slot_spec, t_spec, t_spec,
                   pl.BlockSpec((None, n_heads, tm), lambda b, t: (b, 0, t))),
        out_shape=out_shape,
        input_output_aliases={1 + len(consts) + i: 2 + i for i in range(len(extra))},
        compiler_params=_params("parallel", "parallel"),
        name="in_proj",
    )(x2d, *consts, *extra)


def _s5_kernel(u_ref, h0r_ref, h0i_ref, wb_ref, lbr_ref, lbi_ref, wc_ref, dsk_ref, wglu_ref, bglu_ref,
               s_ref, hr_out, hi_out, hbuf, hst_r, hst_i, uil, sil, *, bsz, tl):
    nblk, cb, sc2 = wb_ref.shape
    sc = sc2 // 2
    interleave = len(u_ref.shape) == 3

    @pl.when(pl.program_id(0) == 0)
    def _():
        hst_r[...] = h0r_ref[...]
        hst_i[...] = h0i_ref[...]

    if interleave:
        for b in range(bsz):
            for j in range(nblk):
                uil[j, pl.ds(b, tl, stride=bsz), :] = u_ref[b, :, cb * j:cb * (j + 1)]
    else:
        for j in range(nblk):
            uil[j] = u_ref[:, cb * j:cb * (j + 1)]

    for j in range(nblk):
        hbuf[:, sc2 * j:sc2 * (j + 1)] = _dot(uil[j].astype(BF16), wb_ref[j])

    for j in range(nblk):
        lre = jnp.broadcast_to(lbr_ref[:, sc * j:sc * (j + 1)], (bsz, sc))
        lim = jnp.broadcast_to(lbi_ref[:, sc * j:sc * (j + 1)], (bsz, sc))
        c_re, c_im = sc2 * j, sc2 * j + sc

        def step(t, carry, lre=lre, lim=lim, c_re=c_re, c_im=c_im):
            hr, hi = carry
            r0 = pl.multiple_of(t * bsz, bsz)
            nr = lre * hr - lim * hi + hbuf[pl.ds(r0, bsz), c_re:c_re + sc]
            ni = lre * hi + lim * hr + hbuf[pl.ds(r0, bsz), c_im:c_im + sc]
            hbuf[pl.ds(r0, bsz), c_re:c_re + sc] = nr
            hbuf[pl.ds(r0, bsz), c_im:c_im + sc] = ni
            return nr, ni

        hr, hi = lax.fori_loop(0, tl, step, (hst_r[:, sc * j:sc * (j + 1)], hst_i[:, sc * j:sc * (j + 1)]),
                               unroll=min(tl, 4))
        hst_r[:, sc * j:sc * (j + 1)] = hr
        hst_i[:, sc * j:sc * (j + 1)] = hi

    ys = []
    for j in range(nblk):
        hj = hbuf[:, sc2 * j:sc2 * (j + 1)].astype(BF16)
        ys.append(_dot(hj, wc_ref[j]) + dsk_ref[:, cb * j:cb * (j + 1)] * uil[j])
    y = jnp.concatenate(ys, axis=1)
    y = 0.5 * y * (1.0 + jnp.tanh(SQRT_2_OVER_PI * (y + 0.044715 * (y * y * y))))
    z = _dot(y.astype(BF16), wglu_ref[...]) + bglu_ref[...]
    out = y * _sigmoid(z)
    if interleave:
        for j in range(nblk):
            sil[j] = out[:, cb * j:cb * (j + 1)]
        for b in range(bsz):
            for j in range(nblk):
                s_ref[b, :, cb * j:cb * (j + 1)] = sil[j, pl.ds(b, tl, stride=bsz), :].astype(s_ref.dtype)
    else:
        s_ref[...] = out.astype(s_ref.dtype)
    hr_out[...] = hst_r[...]
    hi_out[...] = hst_i[...]


def _s5(u, h0r, h0i, wb, lbr, lbi, wc, dsk, wglu, bglu, bsz, tl):
    nstate = h0r.shape[1]
    da = u.shape[-1]
    blk = tl * bsz
    if u.ndim == 3:
        steps = u.shape[1] // tl
        io_spec = pl.BlockSpec((bsz, tl, da), lambda t: (0, t, 0))
    else:
        steps = u.shape[0] // blk
        io_spec = pl.BlockSpec((blk, da), lambda t: (t, 0))
    nblk, cb = wb.shape[0], wb.shape[1]
    return pl.pallas_call(
        functools.partial(_s5_kernel, bsz=bsz, tl=tl),
        grid=(steps,),
        in_specs=[
            io_spec,
            _resident(h0r.shape), _resident(h0i.shape), _resident(wb.shape), _resident(lbr.shape),
            _resident(lbi.shape), _resident(wc.shape), _resident(dsk.shape), _resident_op(wglu),
            _resident(bglu.shape),
        ],
        out_specs=(
            io_spec,
            pl.BlockSpec(h0r.shape, lambda t: (0, 0)),
            pl.BlockSpec(h0i.shape, lambda t: (0, 0)),
        ),
        out_shape=(
            jax.ShapeDtypeStruct(u.shape, BF16),
            jax.ShapeDtypeStruct(h0r.shape, F32),
            jax.ShapeDtypeStruct(h0i.shape, F32),
        ),
        scratch_shapes=[
            pltpu.VMEM((blk, 2 * nstate), F32),
            pltpu.VMEM(h0r.shape, F32),
            pltpu.VMEM(h0i.shape, F32),
            pltpu.VMEM((nblk, blk, cb), F32),
            pltpu.VMEM((nblk, blk, cb), F32),
        ],
        compiler_params=_params("arbitrary"),
        name="s5_mixer",
    )(u, h0r, h0i, wb, lbr, lbi, wc, dsk, _array(wglu), bglu)


def _s5_weights(lam_re, lam_im, log_step, b_re, b_im, c_re, c_im, d_skip):
    n_groups, n_state = lam_re.shape
    p = b_re.shape[-1]
    gpb = LANES // p
    nblk = n_groups // gpb
    step = jnp.exp(log_step)[:, None]
    mag = jnp.exp(lam_re * step)
    lbr = mag * jnp.cos(lam_im * step)
    lbi = mag * jnp.sin(lam_im * step)
    den = lam_re * lam_re + lam_im * lam_im
    fr = ((lbr - 1.0) * lam_re + lbi * lam_im) / den
    fi = (lbi * lam_re - (lbr - 1.0) * lam_im) / den
    bbr = fr[..., None] * b_re - fi[..., None] * b_im
    bbi = fr[..., None] * b_im + fi[..., None] * b_re
    eye = jnp.eye(gpb, dtype=F32)

    def b_block(bb):
        t = bb.reshape(nblk, gpb, n_state, p)
        return jnp.einsum('jgnp,gh->jgphn', t, eye).reshape(nblk, gpb * p, gpb * n_state)

    def c_block(cc):
        t = cc.reshape(nblk, gpb, p, n_state)
        return jnp.einsum('jgpn,gh->jgnhp', t, eye).reshape(nblk, gpb * n_state, gpb * p)

    wb = jnp.concatenate([b_block(bbr), b_block(bbi)], axis=-1).astype(BF16)
    wc = jnp.concatenate([c_block(c_re), -c_block(c_im)], axis=1).astype(BF16)
    return (wb, lbr.reshape(1, n_groups * n_state), lbi.reshape(1, n_groups * n_state), wc,
            d_skip.reshape(1, n_groups * p))


def _cumsum_kernel(x_ref, o_ref):
    rows, length = x_ref.shape
    lane = lax.broadcasted_iota(jnp.int32, (rows, LANES), 1)
    carry = jnp.zeros((rows, 1), F32)
    for c in range(length // LANES):
        x = x_ref[:, LANES * c:LANES * (c + 1)]
        d = 1
        while d < LANES:
            x = x + jnp.where(lane >= d, pltpu.roll(x, d, 1), 0.0)
            d *= 2
        x = x + carry
        o_ref[:, LANES * c:LANES * (c + 1)] = x
        carry = x[:, LANES - 1:LANES]


def _cumsum_rows(x):
    return pl.pallas_call(
        _cumsum_kernel,
        out_shape=jax.ShapeDtypeStruct(x.shape, F32),
        name="logf_cumsum",
    )(x)


def _split3_bf16(x):
    hi = x.astype(BF16)
    r = x - hi.astype(F32)
    mid = r.astype(BF16)
    return hi, mid, (r - mid.astype(F32)).astype(BF16)


def _attn_kernel(q_ref, kT_ref, vT_ref, dc_ref, o_ref, kx_ref, vx_ref, m_ref, acc_ref, *, blk, nsub, hd):
    hp = pl.program_id(1)
    qi = pl.program_id(2)
    d2 = 2 * hd
    n_bias = 16
    length = kT_ref.shape[1]

    @pl.when(qi == 0)
    def _():
        kx_ref[0:d2, :] = kT_ref[...]
        pieces = []
        for hh in range(2):
            pieces += list(_split3_bf16(-LOG2E * dc_ref[pl.ds(2 * hp + hh, 1), :]))
        pieces.append(jnp.zeros((n_bias - len(pieces), length), BF16))
        kx_ref[d2:d2 + n_bias, :] = jnp.concatenate(pieces, axis=0)
        kx_ref[d2 + n_bias:, :] = jnp.zeros((kx_ref.shape[0] - d2 - n_bias, length), BF16)
        sub = lax.broadcasted_iota(jnp.int32, (d2, 1), 0)
        vt = vT_ref[...]
        for hh in range(2):
            own = (sub >= hd) if hh else (sub < hd)
            vx_ref[hh] = jnp.where(own, vt, jnp.ones_like(vt))

    lane = lax.broadcasted_iota(jnp.int32, (1, d2), 1)
    row = lax.broadcasted_iota(jnp.int32, (blk, blk), 0)
    col = lax.broadcasted_iota(jnp.int32, (blk, blk), 1)
    qx = []
    for sb in range(nsub):
        q = q_ref[blk * sb:blk * (sb + 1), :]
        for hh in range(2):
            in_head = (lane >= hd) if hh else (lane < hd)
            sel = (lane >= 3 * hh) & (lane < 3 * hh + 3)
            qx.append(jnp.concatenate(
                [jnp.where(in_head, q, jnp.zeros_like(q)),
                 jnp.broadcast_to(jnp.where(sel, 1.0, 0.0).astype(BF16), (blk, d2))], axis=1))

    m_ref[...] = jnp.full(m_ref.shape, MASKED, F32)
    acc_ref[...] = jnp.zeros(acc_ref.shape, F32)

    def step(kb, carry, first_sub, diag):
        ks = pl.multiple_of(kb * blk, blk)
        kblk = kx_ref[:, pl.ds(ks, blk)]
        for ch in range(2 * first_sub, 2 * nsub):
            s = _dot(qx[ch], kblk)
            if diag and ch // 2 == first_sub:
                s = jnp.where(col <= row, s, MASKED)
            m = m_ref[ch]
            m_new = jnp.maximum(m, jnp.max(s, axis=-1, keepdims=True))
            m_ref[ch] = m_new
            p = jnp.exp2(s - jnp.concatenate([m_new] * (blk // d2), axis=1)).astype(BF16)
            acc_ref[ch] = jnp.exp2(m - m_new) * acc_ref[ch] + _dot_nt(p, vx_ref[ch % 2, :, pl.ds(ks, blk)])
        return carry

    lax.fori_loop(0, nsub * qi, functools.partial(step, first_sub=0, diag=False), 0)
    for sb in range(nsub):
        step(nsub * qi + sb, 0, sb, True)
    for sb in range(nsub):
        acc0, acc1 = acc_ref[2 * sb], acc_ref[2 * sb + 1]
        o_ref[blk * sb:blk * (sb + 1), :] = jnp.where(
            lane < hd, acc0 / pltpu.roll(acc0, hd, 1), acc1 / pltpu.roll(acc1, hd, 1)).astype(o_ref.dtype)


def _attn_prompt(q, kTb, vTb, dc, blk, nsub):
    nb, d, length = kTb.shape
    n_heads = dc.shape[1]
    hd = d // n_heads
    nq = length // (blk * nsub)
    return pl.pallas_call(
        functools.partial(_attn_kernel, blk=blk, nsub=nsub, hd=hd),
        grid=(nb, n_heads // 2, nq),
        in_specs=[
            pl.BlockSpec((blk * nsub, 2 * hd), lambda b, hp, qi: (b * nq + qi, hp)),
            pl.BlockSpec((None, 2 * hd, length), lambda b, hp, qi: (b, hp, 0)),
            pl.BlockSpec((None, 2 * hd, length), lambda b, hp, qi: (b, hp, 0)),
            pl.BlockSpec((None, n_heads, length), lambda b, hp, qi: (b, 0, 0)),
        ],
        out_specs=pl.BlockSpec((blk * nsub, 2 * hd), lambda b, hp, qi: (b * nq + qi, hp)),
        out_shape=jax.ShapeDtypeStruct(q.shape, BF16),
        scratch_shapes=[pltpu.VMEM((4 * hd, length), BF16), pltpu.VMEM((2, 2 * hd, length), BF16),
                        pltpu.VMEM((2 * nsub, blk, 2 * hd), F32), pltpu.VMEM((2 * nsub, blk, 2 * hd), F32)],
        compiler_params=_params("parallel", "parallel", "arbitrary"),
        name="fox_prompt",
    )(q, kTb, vTb, dc)


def _decode_kernel(pt_ref, q_ref, kn_ref, vn_ref, lfn_ref, *refs, pages_per_step, hd):
    del pt_ref
    g = pages_per_step
    k_refs, v_refs, lf_refs = refs[:g], refs[g:2 * g], refs[2 * g:3 * g]
    o_ref, q8_ref, m_ref, l_ref, acc_ref, suf_ref = refs[3 * g:]
    n_heads, d = q8_ref.shape
    s_idx = pl.program_id(1)
    head_of_lane = lax.broadcasted_iota(jnp.int32, (n_heads, d), 1) // hd
    head_of_row = lax.broadcasted_iota(jnp.int32, (n_heads, d), 0)
    own = head_of_lane == head_of_row

    @pl.when(s_idx == 0)
    def _():
        q8 = jnp.where(own, jnp.broadcast_to(q_ref[0], (n_heads, d)), 0.0)
        q8_ref[...] = q8.astype(BF16)
        m_ref[...] = jnp.sum(q8 * kn_ref[0], axis=-1, keepdims=True)
        l_ref[...] = jnp.ones(l_ref.shape, F32)
        acc_ref[...] = jnp.broadcast_to(vn_ref[0], (n_heads, d))
        suf_ref[...] = lfn_ref[0]

    lane = lax.broadcasted_iota(jnp.int32, (n_heads, LANES), 1)
    q8 = q8_ref[...]
    m, l, acc, suf = m_ref[...], l_ref[...], acc_ref[...], suf_ref[...]
    logits = []
    for i in range(g):
        lf = lf_refs[i][0, 0]
        x = lf
        sh = 1
        while sh < LANES:
            x = x + jnp.where(lane < LANES - sh, pltpu.roll(x, LANES - sh, 1), 0.0)
            sh *= 2
        bias = (x - lf) + suf
        suf = suf + x[:, 0:1]
        kt = k_refs[i][0, 0].reshape(d, LANES).astype(BF16)
        logits.append(_dot(q8, kt) + LOG2E * bias)
    s = jnp.concatenate(logits, axis=1)
    m_new = jnp.maximum(m, jnp.max(s, axis=-1, keepdims=True))
    alpha = jnp.exp2(m - m_new)
    p = jnp.exp2(s - m_new)
    l = alpha * l + jnp.sum(p, axis=-1, keepdims=True)
    p = p.astype(BF16)
    pv = None
    for i in range(g):
        vt = v_refs[i][0, 0].reshape(d, LANES).astype(BF16)
        term = _dot_nt(p[:, LANES * i:LANES * (i + 1)], vt)
        pv = term if pv is None else pv + term
    acc = alpha * acc + pv
    m_ref[...], l_ref[...], acc_ref[...], suf_ref[...] = m_new, l, acc, suf

    @pl.when(s_idx == pl.num_programs(1) - 1)
    def _():
        o_ref[0] = jnp.sum(jnp.where(own, acc / l, 0.0), axis=0, keepdims=True).astype(o_ref.dtype)


def _attn_decode(q, k_new, v_new, lf_new, ck_t, cv_t, clf_t, page_table, layer, pages_per_step):
    db, d = q.shape
    n_heads, hd, page = ck_t.shape[2:]
    n_pages = page_table.shape[1]
    g = pages_per_step
    steps = n_pages // g

    def page_of(b, s, pt, i):
        return pt[b, n_pages - 1 - (s * g + i)]

    kv_specs = [pl.BlockSpec((1, 1, n_heads, hd, page),
                             lambda b, s, pt, i=i: (layer, page_of(b, s, pt, i), 0, 0, 0)) for i in range(g)]
    lf_specs = [pl.BlockSpec((1, 1, n_heads, page),
                             lambda b, s, pt, i=i: (layer, page_of(b, s, pt, i), 0, 0)) for i in range(g)]
    row_spec = pl.BlockSpec((1, 1, d), lambda b, s, pt: (b, 0, 0))
    grid_spec = pltpu.PrefetchScalarGridSpec(
        num_scalar_prefetch=1,
        grid=(db, steps),
        in_specs=[row_spec, row_spec, row_spec, pl.BlockSpec((1, n_heads, 1), lambda b, s, pt: (b, 0, 0))]
                 + kv_specs + kv_specs + lf_specs,
        out_specs=row_spec,
        scratch_shapes=[
            pltpu.VMEM((n_heads, d), BF16),
            pltpu.VMEM((n_heads, 1), F32),
            pltpu.VMEM((n_heads, 1), F32),
            pltpu.VMEM((n_heads, d), F32),
            pltpu.VMEM((n_heads, 1), F32),
        ],
    )
    out = pl.pallas_call(
        functools.partial(_decode_kernel, pages_per_step=g, hd=hd),
        grid_spec=grid_spec,
        out_shape=jax.ShapeDtypeStruct((db, 1, d), BF16),
        compiler_params=_params("parallel", "arbitrary"),
        name="fox_decode",
    )(page_table, q.reshape(db, 1, d), k_new.reshape(db, 1, d), v_new.reshape(db, 1, d),
      lf_new.reshape(db, n_heads, 1), *([ck_t] * g), *([cv_t] * g), *([clf_t] * g))
    return out.reshape(db, d)


def _pw1_kernel(x_ref, g_ref, w_ref, b_ref, o_ref):
    h = _rms_rows(x_ref[...], g_ref[...]).astype(BF16)
    z = _dot(h, w_ref[...]) + b_ref[...]
    d = o_ref.shape[1]
    o_ref[...] = z[:, :d] * _sigmoid(z[:, d:])


def _pw1(x2d, tm, g, w, b):
    rows, dm = x2d.shape
    dc = w.shape[1] // 2
    return pl.pallas_call(
        _pw1_kernel,
        grid=(rows // tm,),
        in_specs=[pl.BlockSpec((tm, dm), lambda i: (i, 0)), _resident(g.shape), _resident_op(w),
                  _resident(b.shape)],
        out_specs=pl.BlockSpec((tm, dc), lambda i: (i, 0)),
        out_shape=jax.ShapeDtypeStruct((rows, dc), F32),
        compiler_params=_params("parallel"),
        name="conv_pw1_glu",
    )(x2d, g, _array(w), b)


def _ln_swish(y, g, b):
    mu = jnp.mean(y, axis=-1, keepdims=True)
    yc = y - mu
    y = yc * lax.rsqrt(jnp.mean(yc * yc, axis=-1, keepdims=True) + EPS) * g + b
    return y * _sigmoid(y)


def _conv_kernel(g_ref, w_ref, bdw_ref, lng_ref, lnb_ref, o_ref, buf, ybuf, *, tl, rg, width):
    t = pl.program_id(0)
    bsz, _, dch = g_ref.shape
    hist = width - 1
    nch = dch // LANES
    jg = min(8, rg)

    @pl.when(t == 0)
    def _():
        buf[:, 0:hist * bsz, :] = jnp.zeros((nch, hist * bsz, LANES), F32)

    @pl.when(t > 0)
    def _():
        buf[:, 0:hist * bsz, :] = buf[:, tl * bsz:(tl + hist) * bsz, :]

    for b in range(bsz):
        for c in range(nch):
            buf[c, pl.ds(hist * bsz + b, tl, stride=bsz), :] = g_ref[b, :, LANES * c:LANES * (c + 1)]

    def chunk(ci, carry):
        r0 = pl.multiple_of(ci * (rg * bsz), rg * bsz)

        def lane_chunk(c, carry2):
            bias = jnp.broadcast_to(bdw_ref[c], (bsz, LANES))
            for j0 in range(0, rg, jg):
                accs = [bias] * jg
                for k in range(width):
                    tap = jnp.broadcast_to(w_ref[c, k:k + 1, :], (bsz, LANES))
                    for j in range(jg):
                        accs[j] = accs[j] + tap * buf[c, pl.ds(r0 + (j0 + j + k) * bsz, bsz), :]
                for j in range(jg):
                    ybuf[c, pl.ds(r0 + (j0 + j) * bsz, bsz), :] = accs[j]
            return carry2

        lax.fori_loop(0, nch, lane_chunk, 0)
        rows = pl.ds(r0, rg * bsz)
        y = ybuf[:, rows, :]
        mu = jnp.sum(jnp.sum(y, axis=0), axis=-1, keepdims=True) / dch
        yc = y - mu[None]
        var = jnp.sum(jnp.sum(yc * yc, axis=0), axis=-1, keepdims=True) / dch
        y = yc * lax.rsqrt(var + EPS)[None] * lng_ref[...] + lnb_ref[...]
        ybuf[:, rows, :] = y * _sigmoid(y)
        return carry

    lax.fori_loop(0, tl // rg, chunk, 0)
    for b in range(bsz):
        for c in range(nch):
            o_ref[b, :, LANES * c:LANES * (c + 1)] = ybuf[c, pl.ds(b, tl, stride=bsz), :].astype(o_ref.dtype)


def _conv_prompt(g3d, tl, w_pad, bdw, lng, lnb, width):
    nb, length, dch = g3d.shape
    hist = width - 1
    nch = dch // LANES
    lng, lnb, bdw = (v.reshape(nch, 1, LANES) for v in (lng, lnb, bdw))
    w_pad = jnp.transpose(w_pad.reshape(-1, nch, LANES), (1, 0, 2))
    return pl.pallas_call(
        functools.partial(_conv_kernel, tl=tl, rg=min(8, tl), width=width),
        grid=(length // tl,),
        in_specs=[pl.BlockSpec((nb, tl, dch), lambda t: (0, t, 0)), _resident(w_pad.shape),
                  _resident(bdw.shape), _resident(lng.shape), _resident(lnb.shape)],
        out_specs=pl.BlockSpec((nb, tl, dch), lambda t: (0, t, 0)),
        out_shape=jax.ShapeDtypeStruct((nb, length, dch), BF16),
        scratch_shapes=[pltpu.VMEM((nch, (hist + tl) * nb, LANES), F32), pltpu.VMEM((nch, tl * nb, LANES), F32)],
        compiler_params=_params("arbitrary"),
        name="conv_dw_prompt",
    )(g3d, w_pad, bdw, lng, lnb)


def _conv_step_kernel(hist_ref, g_ref, w_ref, bdw_ref, lng_ref, lnb_ref, o_ref, *, width):
    acc = bdw_ref[...] + w_ref[width - 1:width, :] * g_ref[...]
    for k in range(width - 1):
        acc = acc + w_ref[k:k + 1, :] * hist_ref[k]
    o_ref[...] = _ln_swish(acc, lng_ref[...], lnb_ref[...]).astype(o_ref.dtype)


def _conv_sample(hist_t, g2d, w_pad, bdw, lng, lnb, width):
    return pl.pallas_call(
        functools.partial(_conv_step_kernel, width=width),
        out_shape=jax.ShapeDtypeStruct(g2d.shape, BF16),
        compiler_params=pltpu.CompilerParams(vmem_limit_bytes=VMEM_LIMIT),
        name="conv_dw_sample",
    )(hist_t, g2d, w_pad, bdw, lng, lnb)


def _mixffn_kernel(*refs, n_mix, has_bias, ff_chunks):
    x_ref = refs[0]
    acts = refs[1:1 + n_mix]
    wo_ref = refs[1 + n_mix]
    pos = 2 + n_mix
    bias_ref = refs[pos] if has_bias else None
    pos += int(has_bias)
    gn_ref, wg_ref, wu_ref, wd_ref, o_ref = refs[pos:pos + 5]
    mixed = acts[0][...] if n_mix == 1 else jnp.concatenate([a[...] for a in acts], axis=1)
    x1 = x_ref[...] + _dot(mixed, wo_ref[...])
    if has_bias:
        x1 = x1 + bias_ref[...]
    h = _rms_rows(x1, gn_ref[...]).astype(BF16)
    acc = x1
    for c0, c1 in ff_chunks:
        gate = _dot(h, wg_ref[:, c0:c1])
        up = _dot(h, wu_ref[:, c0:c1])
        acc = acc + _dot((gate * _sigmoid(gate) * up).astype(BF16), wd_ref[c0:c1, :])
    o_ref[...] = acc


def _ff_chunks(dff, width=1024):
    return tuple((c, min(c + width, dff)) for c in range(0, dff, width))


def _mix_ffn(x2d, nb, tm, mix_acts, wo, bias, gn, wg, wu, wd):
    rows, dm = x2d.shape
    lr = rows // nb
    nt = lr // tm
    in_specs = [pl.BlockSpec((tm, dm), lambda b, t: (b * nt + t, 0))]
    for a in mix_acts:
        if a.shape[0] == rows:
            in_specs.append(pl.BlockSpec((tm, a.shape[1]), lambda b, t: (b * nt + t, 0)))
        else:
            in_specs.append(pl.BlockSpec((tm, a.shape[1] // nb), lambda b, t: (t, b)))
    consts = [wo] + ([bias] if bias is not None else []) + [gn, wg, wu, wd]
    in_specs += [_resident_op(c) for c in consts]
    return pl.pallas_call(
        functools.partial(_mixffn_kernel, n_mix=len(mix_acts), has_bias=bias is not None,
                          ff_chunks=_ff_chunks(wg.shape[1])),
        grid=(nb, nt),
        in_specs=in_specs,
        out_specs=pl.BlockSpec((tm, dm), lambda b, t: (b * nt + t, 0)),
        out_shape=jax.ShapeDtypeStruct((rows, dm), F32),
        compiler_params=_params("parallel", "parallel"),
        name="mix_ffn",
    )(x2d, *mix_acts, *[_array(c) for c in consts])


def kernel(x_prompt, x_sample, cache_k, cache_v, cache_logf, page_table, state_ssm_re, state_ssm_im, state_conv, norm_mix, norm_ffn, w_in, b_f, q_norm, k_norm, lam_re, lam_im, log_step, b_re, b_im, c_re, c_im, d_skip, w_glu, b_glu, w_out, w_pw1, b_pw1, w_dw, b_dw, ln_g, ln_b, w_pw2, b_pw2, w_gate, w_up, w_down):
    nb, length, dm = x_prompt.shape
    db = x_sample.shape[0]
    depth = norm_mix.shape[0]
    n_heads, hd = cache_k.shape[3], cache_k.shape[4]
    d_b = n_heads * hd
    d_a = w_in.shape[2] - 3 * d_b - n_heads
    n_state = state_ssm_re.shape[2] * state_ssm_re.shape[3]
    width = w_dw.shape[1]
    tm_p = min(512, length)
    blk_attn = min(512, length)
    nsub_attn = 4 if length % (4 * blk_attn) == 0 else (2 if length % (2 * blk_attn) == 0 else 1)
    tl_s5 = min(64, length)
    tl_conv = min(64, length)
    pages_per_step = min(32, page_table.shape[1])

    ck_t = jnp.transpose(cache_k, (0, 1, 3, 4, 2))
    cv_t = jnp.transpose(cache_v, (0, 1, 3, 4, 2))
    clf_t = jnp.transpose(cache_logf, (0, 1, 3, 2))

    seg = jnp.kron(jnp.eye(n_heads, dtype=F32), jnp.full((hd, hd), 1.0 / hd, F32)).astype(BF16)
    row = lambda v: v.reshape(1, -1).astype(F32)

    yp = x_prompt.reshape(nb * length, dm)
    ys = x_sample.reshape(db, dm)
    n_even = (depth + 1) // 2
    kv_stacks = None
    outs = {k: [] for k in ("lfp", "srp", "sip", "cvp", "ks", "vs", "lfs", "srs", "sis", "cvs")}

    wg_all, wu_all, wd_all, wglu_all, wout_all, wpw1_all, wpw2_all = (
        w.astype(BF16) for w in (w_gate, w_up, w_down, w_glu, w_out, w_pw1, w_pw2))

    for layer in range(depth):
        gn = row(norm_ffn[layer])
        wg, wu, wd = _LayerOf(wg_all, layer), _LayerOf(wu_all, layer), _LayerOf(wd_all, layer)
        if layer % 2 == 0:
            e = layer // 2
            w = w_in[e]
            wuq = w[:, :d_a + d_b].astype(BF16)
            wkvT = w[:, d_a + d_b:d_a + 3 * d_b].T.astype(BF16)
            wfT = jnp.zeros((16, dm), F32).at[:n_heads].set(w[:, d_a + 3 * d_b:].T).astype(BF16)
            proj = (row(norm_mix[layer]), wuq, wkvT, wfT, b_f[e].reshape(n_heads, 1),
                    row(jnp.tile(q_norm[e], n_heads) * (hd ** -0.5 * LOG2E)), k_norm[e].reshape(hd, 1), seg, n_heads)
            s5w = _s5_weights(lam_re[e], lam_im[e], log_step[e], b_re[e], b_im[e], c_re[e], c_im[e], d_skip[e])
            glu = (_LayerOf(wglu_all, e), row(b_glu[e]))
            wo = _LayerOf(wout_all, e)

            u, q, kT_all, vT_all, kTb, vTb, lfT = _in_proj(yp, nb, tm_p, *proj, slot=e, n_slots=n_even,
                                                           stacks=kv_stacks)
            kv_stacks = (kT_all, vT_all)
            zeros = jnp.zeros((nb, n_state), F32)
            s_out, hr, hi = _s5(u.reshape(nb, length, d_a), zeros, zeros, *s5w, *glu, bsz=nb, tl=tl_s5)
            dc = _cumsum_rows(lfT.reshape(nb * n_heads, length)).reshape(nb, n_heads, length)
            a_out = _attn_prompt(q, kTb, vTb, dc, blk_attn, nsub_attn)
            yp = _mix_ffn(yp, nb, tm_p, [s_out.reshape(nb * length, d_a), a_out], wo, None,
                          gn, wg, wu, wd)
            outs["lfp"].append(jnp.transpose(lfT, (0, 2, 1)))
            outs["srp"].append(hr.reshape(nb, -1, state_ssm_re.shape[3]))
            outs["sip"].append(hi.reshape(nb, -1, state_ssm_re.shape[3]))

            u, q, kT, vT, _, _, lfT = _in_proj(ys, 1, db, *proj)
            k_new, v_new, lf_new = kT[0, 0].T, vT[0, 0].T, lfT[0].T
            s_out, hr, hi = _s5(u, state_ssm_re[e].reshape(db, n_state), state_ssm_im[e].reshape(db, n_state),
                                *s5w, *glu, bsz=db, tl=1)
            a_out = _attn_decode(q.astype(F32), k_new, v_new, lf_new, ck_t, cv_t, clf_t, page_table, e,
                                 pages_per_step)
            ys = _mix_ffn(ys, 1, db, [s_out, a_out], wo, None, gn, wg, wu, wd)
            outs["ks"].append(k_new.reshape(db, 1, n_heads, hd))
            outs["vs"].append(v_new.reshape(db, 1, n_heads, hd))
            outs["lfs"].append(lf_new.reshape(db, 1, n_heads))
            outs["srs"].append(hr.reshape(db, -1, state_ssm_re.shape[3]))
            outs["sis"].append(hi.reshape(db, -1, state_ssm_re.shape[3]))
        else:
            o = layer // 2
            gm = row(norm_mix[layer])
            w1, b1 = _LayerOf(wpw1_all, o), row(b_pw1[o])
            w_pad = jnp.zeros((-(-width // 8) * 8, dm), F32).at[:width].set(w_dw[o])
            dw = (w_pad, row(b_dw[o]), row(ln_g[o]), row(ln_b[o]))
            w2, b2 = _LayerOf(wpw2_all, o), row(b_pw2[o])

            g = _pw1(yp, tm_p, gm, w1, b1)
            act = _conv_prompt(g.reshape(nb, length, dm), tl_conv, *dw, width=width).reshape(nb * length, dm)
            yp = _mix_ffn(yp, nb, tm_p, [act], w2, b2, gn, wg, wu, wd)
            outs["cvp"].append(g.reshape(nb, length, dm)[:, length - (width - 1):])

            g = _pw1(ys, db, gm, w1, b1)
            hist = state_conv[o]
            act = _conv_sample(jnp.transpose(hist, (1, 0, 2)), g, *dw, width=width)
            ys = _mix_ffn(ys, 1, db, [act], w2, b2, gn, wg, wu, wd)
            outs["cvs"].append(jnp.concatenate([hist[:, 1:], g[:, None, :]], axis=1))

    st = lambda k: jnp.stack(outs[k])
    kv_out = [jnp.transpose(a.reshape(n_even, nb, n_heads, hd, length), (0, 1, 4, 2, 3)) for a in kv_stacks]
    return (yp.reshape(nb, length, dm), ys.reshape(db, 1, dm),
            kv_out[0], kv_out[1], st("lfp"), st("srp"), st("sip"), st("cvp"),
            st("ks"), st("vs"), st("lfs"), st("srs"), st("sis"), st("cvs"))
```
